```python
import math
import jax, jax.numpy as jnp
from jax import lax
import numpy as np

D_MODEL = 1024
BATCH = 32
SEQ = 2048
DEPTH = 1
DEC_BATCH = 128
DEC_SEQ = 4
PAST_LEN = 8192
PAGE_SIZE = 128

R_HEADS = 4
R_DK = 128
R_DV = 256
R_CHUNK = 128
A_HEADS = 4
A_DH = 64
A_DV = 2 * A_DH
Q_BLOCK = 128
N_GROUPS = 4
EXP_PER_GROUP = 8
N_EXPERTS = N_GROUPS * EXP_PER_GROUP
TOP_K = 2
D_EXPERT = 512
MOE_BLOCK = 128

EPS = 1e-6
SUBLN_EPS = 1e-5
NEG_BIG = -1e30

R_QK = R_HEADS * R_DK
R_V = R_HEADS * R_DV
A_QK = A_HEADS * 2 * A_DH
A_V = A_HEADS * A_DV
SPLITS = (R_QK, R_QK, R_V, R_V, A_QK, A_QK, A_V, D_MODEL, D_MODEL)
N_IN = sum(SPLITS)

kernel_name = "hybrid_retention_diffattn_hmoe_step"


def rms_norm(x, g=None, eps=EPS):
    xf = x.astype(jnp.float32)
    y = xf * lax.rsqrt(jnp.mean(xf * xf, axis=-1, keepdims=True) + eps)
    if g is not None:
        y = y * g.astype(jnp.float32)
    return y.astype(x.dtype)


def retention_log_decay():
    return jnp.log1p(-jnp.exp2(-5.0 - jnp.arange(R_HEADS, dtype=jnp.float32)))


def alibi_slopes():
    return jnp.exp2(-8.0 * jnp.arange(1, A_HEADS + 1, dtype=jnp.float32) / A_HEADS)


def project(xn, w_in):
    b, t, _ = xn.shape
    proj = xn @ w_in
    parts = []
    off = 0
    for w in SPLITS:
        parts.append(proj[..., off:off + w])
        off += w
    rq, rk, rv, rg, aq, ak, av, ga, gb = parts
    rq = rq.reshape(b, t, R_HEADS, R_DK)
    rk = rk.reshape(b, t, R_HEADS, R_DK) * (R_DK ** -0.5)
    rv = rv.reshape(b, t, R_HEADS, R_DV)
    aq = aq.reshape(b, t, A_HEADS, 2, A_DH)
    ak = ak.reshape(b, t, A_HEADS, 2, A_DH)
    av = av.reshape(b, t, A_HEADS, A_DV)
    return rq, rk, rv, rg, aq, ak, av, ga, gb


def retention_chunk(s, q, k, v, log_g):
    dt = q.dtype
    s = s.astype(dt)
    c = q.shape[1]
    pos = jnp.arange(c, dtype=jnp.float32)
    dist = pos[:, None] - pos[None, :]
    intra = jnp.where(dist[None] >= 0, jnp.exp(jnp.maximum(dist, 0.0)[None] * log_g[:, None, None]), 0.0).astype(dt)
    in_decay = jnp.exp((pos[:, None] + 1.0) * log_g[None, :]).astype(dt)
    out_decay = jnp.exp((c - 1.0 - pos[:, None]) * log_g[None, :]).astype(dt)
    chunk_decay = jnp.exp(c * log_g).astype(dt)
    scores = jnp.einsum('bihk,bjhk->bhij', q, k) * intra[None]
    o = (jnp.einsum('bhij,bjhv->bihv', scores, v)
         + jnp.einsum('bihk,bhkv->bihv', q * in_decay[None, :, :, None], s))
    s_new = (chunk_decay[None, :, None, None] * s
             + jnp.einsum('bjhk,bjhv->bhkv', k * out_decay[None, :, :, None], v))
    return s_new.astype(dt), o.astype(dt)


def retention_prompt(q, k, v, log_g):
    b, t = q.shape[:2]
    n = t // R_CHUNK

    def to_chunks(a):
        return a.reshape((b, n, R_CHUNK) + a.shape[2:]).swapaxes(0, 1)

    def body(st, xs):
        return retention_chunk(st, xs[0], xs[1], xs[2], log_g)

    s0 = jnp.zeros((b, R_HEADS, R_DK, R_DV), q.dtype)
    s, o = lax.scan(body, s0, (to_chunks(q), to_chunks(k), to_chunks(v)))
    return o.swapaxes(0, 1).reshape(b, t, R_HEADS, R_DV), s


def diff_attn_prompt(q, k, v, lam):
    b, t = q.shape[:2]
    n = t // Q_BLOCK
    slopes = alibi_slopes()
    k_pos = jnp.arange(t, dtype=jnp.int32)
    scale = A_DH ** -0.5
    qb = q.reshape(b, n, Q_BLOCK, A_HEADS, 2, A_DH).swapaxes(0, 1)
    lam_f = lam.astype(jnp.float32)

    def block(args):
        qi, i = args
        q_pos = i * Q_BLOCK + jnp.arange(Q_BLOCK, dtype=jnp.int32)
        s = jnp.einsum('bqhcd,bkhcd->bhcqk', qi, k).astype(jnp.float32) * scale
        dist = (q_pos[:, None] - k_pos[None, :]).astype(jnp.float32)
        s = s - slopes[:, None, None, None] * dist[None, None]
        s = jnp.where(dist[None, None, None] >= 0, s, NEG_BIG)
        p = jax.nn.softmax(s, axis=-1)
        o = jnp.einsum('bhcqk,bkhv->bqhcv', p, v.astype(jnp.float32))
        return (o[..., 0, :] - lam_f * o[..., 1, :]).astype(v.dtype)

    o = lax.map(block, (qb, jnp.arange(n, dtype=jnp.int32)))
    return o.swapaxes(0, 1).reshape(b, t, A_HEADS, A_DV)


def diff_attn_sample(q, k_new, v_new, k_pages, v_pages, page_table, lam):
    tn = q.shape[1]
    n_pages = page_table.shape[1]
    slopes = alibi_slopes()
    q_pos = PAST_LEN + jnp.arange(tn, dtype=jnp.int32)
    scale = A_DH ** -0.5
    qf = q.astype(jnp.float32)

    def scores(kb, k_pos):
        s = jnp.einsum('bqhcd,bkhcd->bhcqk', qf, kb.astype(jnp.float32)) * scale
        dist = (q_pos[:, None] - k_pos[None, :]).astype(jnp.float32)
        s = s - slopes[:, None, None, None] * dist[None, None]
        return jnp.where(dist[None, None, None] >= 0, s, NEG_BIG)

    s = scores(k_new, q_pos)
    m = jnp.max(s, axis=-1)
    p = jnp.exp(s - m[..., None])
    l = jnp.sum(p, axis=-1)
    acc = jnp.einsum('bhcqk,bkhv->bhcqv', p, v_new.astype(jnp.float32))

    def step(carry, xs):
        m, l, acc = carry
        pages, j = xs
        kb = k_pages[pages]
        vb = v_pages[pages]
        s = scores(kb, j * PAGE_SIZE + jnp.arange(PAGE_SIZE, dtype=jnp.int32))
        m_new = jnp.maximum(m, jnp.max(s, axis=-1))
        alpha = jnp.exp(m - m_new)
        p = jnp.exp(s - m_new[..., None])
        l_new = alpha * l + jnp.sum(p, axis=-1)
        acc_new = alpha[..., None] * acc + jnp.einsum('bhcqk,bkhv->bhcqv', p, vb.astype(jnp.float32))
        return (m_new, l_new, acc_new), None

    (m, l, acc), _ = lax.scan(step, (m, l, acc),
                              (page_table.T.astype(jnp.int32), jnp.arange(n_pages, dtype=jnp.int32)))
    o = acc / l[..., None]
    out = o[:, :, 0] - lam.astype(jnp.float32) * o[:, :, 1]
    return out.transpose(0, 2, 1, 3).astype(q.dtype)


def merge_branches(ro, rg, ao, ga, gb, subln_g, lam_init, w_ret_out, w_diff_out, w_out):
    b, t = ro.shape[:2]
    ret = rms_norm(ro).reshape(b, t, R_V) * jax.nn.silu(rg)
    dif = (rms_norm(ao, subln_g, SUBLN_EPS) * (1.0 - lam_init)).reshape(b, t, A_V)
    m = jax.nn.sigmoid(ga) * (ret @ w_ret_out) + jax.nn.sigmoid(gb) * (dif @ w_diff_out)
    return m @ w_out


def expert_dispatch(xf, expert_id, weights, w_up, w_gate, w_down):
    n_tok, d = xf.shape
    n_assign = n_tok * TOP_K
    n_blocks = -(-n_assign // MOE_BLOCK) + N_EXPERTS
    rows = n_blocks * MOE_BLOCK
    flat_e = expert_id.reshape(-1)
    order = jnp.argsort(flat_e)
    sorted_e = flat_e[order]
    counts = jnp.sum(jax.nn.one_hot(flat_e, N_EXPERTS, dtype=jnp.int32), axis=0)
    padded = (counts + MOE_BLOCK - 1) // MOE_BLOCK * MOE_BLOCK
    pad_end = jnp.cumsum(padded)
    pad_start = pad_end - padded
    start = jnp.cumsum(counts) - counts
    dest = pad_start[sorted_e] + jnp.arange(n_assign, dtype=jnp.int32) - start[sorted_e]
    row_tok = jnp.zeros((rows,), jnp.int32).at[dest].set((order // TOP_K).astype(jnp.int32))
    row_w = jnp.zeros((rows,), xf.dtype).at[dest].set(weights.reshape(-1)[order].astype(xf.dtype))
    block_start = jnp.arange(n_blocks, dtype=jnp.int32) * MOE_BLOCK
    block_e = jnp.sum((block_start[:, None] >= pad_end[None, :]).astype(jnp.int32), axis=1)
    block_e = jnp.minimum(block_e, N_EXPERTS - 1).astype(jnp.int32)
    xb = xf[row_tok].reshape(n_blocks, MOE_BLOCK, d)

    def run(args):
        xi, e = args
        h = jax.nn.silu(xi @ w_gate[e]) * (xi @ w_up[e])
        return (h @ w_down[e]).astype(xf.dtype)

    yb = lax.map(run, (xb, block_e)).reshape(rows, d) * row_w[:, None]
    return jnp.zeros_like(xf).at[row_tok].add(yb.astype(xf.dtype))


def hier_moe(x, w_group, b_group, w_expert, b_expert, w_up, w_gate, w_down):
    b, t, d = x.shape
    xf = x.reshape(-1, d)
    n_tok = xf.shape[0]
    tok = jnp.arange(n_tok, dtype=jnp.int32)
    g_logits = (xf @ w_group).astype(jnp.float32) + b_group.astype(jnp.float32)
    g_prob = jax.nn.softmax(g_logits, axis=-1)
    g_sel = jnp.argmax(g_logits, axis=-1).astype(jnp.int32)
    g_w = g_prob[tok, g_sel][:, None]
    e_all = ((xf @ w_expert).astype(jnp.float32) + b_expert.astype(jnp.float32)).reshape(n_tok, N_GROUPS, EXP_PER_GROUP)
    e_logits = e_all[tok, g_sel]
    e_prob = jax.nn.softmax(e_logits, axis=-1)
    top_p, top_i = lax.top_k(e_prob, TOP_K)
    weights = g_w * top_p / jnp.sum(top_p, axis=-1, keepdims=True)
    expert_id = (g_sel[:, None] * EXP_PER_GROUP + top_i.astype(jnp.int32)).astype(jnp.int32)
    y = expert_dispatch(xf, expert_id, weights, w_up, w_gate, w_down)
    return y.reshape(b, t, d)


def setup_inputs(seed: int = 0) -> dict:
    key = jax.random.key(seed)
    ks = jax.random.split(key, 25)
    n_pages = PAST_LEN // PAGE_SIZE
    n_used = DEC_BATCH * n_pages
    n_phys = n_used + max(1, n_used // 4)
    f32 = jnp.float32

    def nrm(k, shape, scale):
        return jax.random.normal(k, shape, f32) * scale

    page_table = jax.random.permutation(ks[5], n_phys)[:n_used].reshape(DEC_BATCH, n_pages).astype(jnp.int32)
    return {
        "x_prompt": nrm(ks[0], (BATCH, SEQ, D_MODEL), 1.0),
        "x_sample": nrm(ks[1], (DEC_BATCH, DEC_SEQ, D_MODEL), 1.0),
        "state_ret": nrm(ks[2], (DEPTH, DEC_BATCH, R_HEADS, R_DK, R_DV), 0.5),
        "cache_k": nrm(ks[3], (DEPTH, n_phys, PAGE_SIZE, A_HEADS, 2, A_DH), 1.0),
        "cache_v": nrm(ks[4], (DEPTH, n_phys, PAGE_SIZE, A_HEADS, A_DV), 1.0),
        "page_table": page_table,
        "norm_mix": 1.0 + nrm(ks[6], (DEPTH, D_MODEL), 0.02),
        "w_in": nrm(ks[7], (DEPTH, D_MODEL, N_IN), D_MODEL ** -0.5),
        "lambda_q1": nrm(ks[8], (DEPTH, A_DH), 0.1),
        "lambda_k1": nrm(ks[9], (DEPTH, A_DH), 0.1),
        "lambda_q2": nrm(ks[10], (DEPTH, A_DH), 0.1),
        "lambda_k2": nrm(ks[11], (DEPTH, A_DH), 0.1),
        "subln_g": 1.0 + nrm(ks[12], (DEPTH, A_DV), 0.02),
        "w_ret_out": nrm(ks[13], (DEPTH, R_V, D_MODEL), R_V ** -0.5),
        "w_diff_out": nrm(ks[14], (DEPTH, A_V, D_MODEL), A_V ** -0.5),
        "w_out": nrm(ks[15], (DEPTH, D_MODEL, D_MODEL), D_MODEL ** -0.5),
        "norm_ffn": 1.0 + nrm(ks[16], (DEPTH, D_MODEL), 0.02),
        "w_group": nrm(ks[17], (DEPTH, D_MODEL, N_GROUPS), D_MODEL ** -0.5),
        "b_group": nrm(ks[18], (DEPTH, N_GROUPS), 0.01),
        "w_expert": nrm(ks[19], (DEPTH, D_MODEL, N_EXPERTS), D_MODEL ** -0.5),
        "b_expert": nrm(ks[20], (DEPTH, N_EXPERTS), 0.01),
        "w_up": nrm(ks[21], (DEPTH, N_EXPERTS, D_MODEL, D_EXPERT), D_MODEL ** -0.5),
        "w_gate": nrm(ks[22], (DEPTH, N_EXPERTS, D_MODEL, D_EXPERT), D_MODEL ** -0.5),
        "w_down": nrm(ks[23], (DEPTH, N_EXPERTS, D_EXPERT, D_MODEL), D_EXPERT ** -0.5),
        "norm_final": 1.0 + nrm(ks[24], (D_MODEL,), 0.02),
    }


def reference(x_prompt, x_sample, state_ret, cache_k, cache_v, page_table,
              norm_mix, w_in, lambda_q1, lambda_k1, lambda_q2, lambda_k2, subln_g,
              w_ret_out, w_diff_out, w_out, norm_ffn, w_group, b_group, w_expert, b_expert,
              w_up, w_gate, w_down, norm_final):
    log_g = retention_log_decay()
    hp, hs = x_prompt, x_sample
    sp_l, ss_l, kp_l, vp_l, ksm_l, vsm_l = [], [], [], [], [], []
    for l in range(DEPTH):
        lam_init = 0.8 - 0.6 * math.exp(-0.3 * l)
        lam = (jnp.exp(jnp.sum(lambda_q1[l].astype(jnp.float32) * lambda_k1[l].astype(jnp.float32)))
               - jnp.exp(jnp.sum(lambda_q2[l].astype(jnp.float32) * lambda_k2[l].astype(jnp.float32)))
               + lam_init)

        rq, rk, rv, rg, aq, ak, av, ga, gb = project(rms_norm(hp, norm_mix[l]), w_in[l])
        ro, s_p = retention_prompt(rq, rk, rv, log_g)
        ao = diff_attn_prompt(aq, ak, av, lam)
        hp = hp + merge_branches(ro, rg, ao, ga, gb, subln_g[l], lam_init, w_ret_out[l], w_diff_out[l], w_out[l])
        hp = hp + hier_moe(rms_norm(hp, norm_ffn[l]), w_group[l], b_group[l], w_expert[l], b_expert[l],
                           w_up[l], w_gate[l], w_down[l])
        sp_l.append(s_p)
        kp_l.append(ak)
        vp_l.append(av)

        rq, rk, rv, rg, aq, ak, av, ga, gb = project(rms_norm(hs, norm_mix[l]), w_in[l])
        s_s, ro = retention_chunk(state_ret[l], rq, rk, rv, log_g)
        ao = diff_attn_sample(aq, ak, av, cache_k[l], cache_v[l], page_table, lam)
        hs = hs + merge_branches(ro, rg, ao, ga, gb, subln_g[l], lam_init, w_ret_out[l], w_diff_out[l], w_out[l])
        hs = hs + hier_moe(rms_norm(hs, norm_ffn[l]), w_group[l], b_group[l], w_expert[l], b_expert[l],
                           w_up[l], w_gate[l], w_down[l])
        ss_l.append(s_s)
        ksm_l.append(ak)
        vsm_l.append(av)

    y_prompt = rms_norm(hp, norm_final)
    y_sample = rms_norm(hs, norm_final)
    return (y_prompt, y_sample, jnp.stack(sp_l), jnp.stack(ss_l), jnp.stack(kp_l), jnp.stack(vp_l),
            jnp.stack(ksm_l), jnp.stack(vsm_l))
```

```python
import functools

import numpy as np
import jax
import jax.numpy as jnp
from jax import lax
from jax.experimental import pallas as pl
from jax.experimental.pallas import tpu as pltpu

D_MODEL = 1024
R_HEADS, R_DK, R_DV, R_CHUNK = 4, 128, 256, 128
A_HEADS, A_DH, A_DV = 4, 64, 128
N_GROUPS, EXP_PER_GROUP, TOP_K, D_EXPERT = 4, 8, 2, 512
N_EXPERTS = N_GROUPS * EXP_PER_GROUP
PAGE_SIZE = 128
EPS, SUBLN_EPS, NEG_BIG = 1e-6, 1e-5, -1e30

R_QK, R_V = R_HEADS * R_DK, R_HEADS * R_DV
A_QK, A_V = A_HEADS * 2 * A_DH, A_HEADS * A_DV
OFF_RQ, OFF_RK, OFF_RV, OFF_RG = 0, R_QK, 2 * R_QK, 2 * R_QK + R_V
OFF_AQ = OFF_RG + R_V
OFF_AK, OFF_AV = OFF_AQ + A_QK, OFF_AQ + 2 * A_QK
OFF_GA = OFF_AV + A_V
OFF_GB = OFF_GA + D_MODEL
N_IN = OFF_GB + D_MODEL

PROJ_BN = 512
ATTN_T = 256
PAGES_PER_STEP = 16
MOE_TM = 256
VMEM_LIMIT = 56 * 1024 * 1024

F32, BF16 = jnp.float32, jnp.bfloat16
_NT = (((1,), (1,)), ((), ()))
_TN = (((0,), (0,)), ((), ()))


def _cparams(*sem):
    return pltpu.CompilerParams(dimension_semantics=sem, vmem_limit_bytes=VMEM_LIMIT)


def _log_decay():
    return np.log1p(-np.exp2(-5.0 - np.arange(R_HEADS, dtype=np.float64)))


def _alibi_slopes():
    return np.exp2(-8.0 * np.arange(1, A_HEADS + 1, dtype=np.float64) / A_HEADS)


def _proj_kernel(x_ref, g_ref, w_ref, p_ref, k_ref, v_ref, xn_ref):
    j = pl.program_id(1)

    @pl.when(j == 0)
    def _():
        x = x_ref[...]
        ms = jnp.mean(x * x, axis=-1, keepdims=True)
        xn_ref[...] = (x * lax.rsqrt(ms + EPS) * g_ref[...]).astype(BF16)

    acc = jnp.dot(xn_ref[...], w_ref[...], preferred_element_type=F32)
    p_ref[...] = acc.astype(BF16)

    @pl.when(j == OFF_AK // PROJ_BN)
    def _():
        k_ref[...] = acc

    @pl.when(j == OFF_AV // PROJ_BN)
    def _():
        v_ref[...] = acc


def _proj(x, gain, w_bf16):
    m = x.shape[0]
    bm = min(m, 1024)
    return pl.pallas_call(
        _proj_kernel,
        grid=(m // bm, N_IN // PROJ_BN),
        in_specs=[pl.BlockSpec((bm, D_MODEL), lambda i, j: (i, 0)),
                  pl.BlockSpec((1, D_MODEL), lambda i, j: (0, 0)),
                  pl.BlockSpec((D_MODEL, PROJ_BN), lambda i, j: (0, j))],
        out_specs=[pl.BlockSpec((bm, PROJ_BN), lambda i, j: (i, j)),
                   pl.BlockSpec((bm, A_QK), lambda i, j: (i, 0)),
                   pl.BlockSpec((bm, A_V), lambda i, j: (i, 0))],
        out_shape=[jax.ShapeDtypeStruct((m, N_IN), BF16),
                   jax.ShapeDtypeStruct((m, A_QK), F32),
                   jax.ShapeDtypeStruct((m, A_V), F32)],
        scratch_shapes=[pltpu.VMEM((bm, D_MODEL), BF16)],
        compiler_params=_cparams("parallel", "arbitrary"),
        name="proj",
    )(x, gain.reshape(1, D_MODEL), w_bf16)


def _ret_consts(c, rows_per_seq):
    lg = _log_decay()
    pos = np.arange(c, dtype=np.float64)
    dist = pos[:, None] - pos[None, :]
    n = c * rows_per_seq
    intra = np.zeros((R_HEADS, n, n), np.float32)
    qdec = np.zeros((n, R_QK), np.float32)
    kdec = np.zeros((n, R_QK), np.float32)
    for h in range(R_HEADS):
        blk = np.where(dist >= 0, np.exp(np.maximum(dist, 0.0) * lg[h]), 0.0)
        for s in range(rows_per_seq):
            intra[h, s * c:(s + 1) * c, s * c:(s + 1) * c] = blk
            qdec[s * c:(s + 1) * c, h * R_DK:(h + 1) * R_DK] = np.exp((pos + 1.0) * lg[h])[:, None]
            kdec[s * c:(s + 1) * c, h * R_DK:(h + 1) * R_DK] = (
                np.exp((c - 1.0 - pos) * lg[h])[:, None] * R_DK ** -0.5)
    cdec = [float(np.float32(np.exp(c * lg[h]))) for h in range(R_HEADS)]
    return jnp.asarray(intra), jnp.asarray(qdec), jnp.asarray(kdec), cdec


def _gated_norm(o, g):
    ms = jnp.mean(o * o, axis=-1, keepdims=True)
    return (o * lax.rsqrt(ms + EPS)) * (g * jax.nn.sigmoid(g))


def _ret_prompt_kernel(q_ref, k_ref, v_ref, rg_ref, intra_ref, qdec_ref, kdec_ref,
                       ret_ref, sout_ref, s_scr, *, cdec):
    c = pl.program_id(1)

    @pl.when(c == 0)
    def _():
        s_scr[...] = jnp.zeros_like(s_scr)

    q = q_ref[...]
    kf = k_ref[...].astype(F32)
    qd = (q.astype(F32) * qdec_ref[...]).astype(BF16)
    kd = (kf * kdec_ref[...]).astype(BF16)
    ks = (kf * (R_DK ** -0.5)).astype(BF16)
    for h in range(R_HEADS):
        hk = slice(h * R_DK, (h + 1) * R_DK)
        hv = slice(h * R_DV, (h + 1) * R_DV)
        vh = v_ref[:, hv]
        s_old = s_scr[h]
        sc = lax.dot_general(q[:, hk], ks[:, hk], _NT, preferred_element_type=F32) * intra_ref[h]
        o = (jnp.dot(sc.astype(BF16), vh, preferred_element_type=F32)
             + jnp.dot(qd[:, hk], s_old.astype(BF16), preferred_element_type=F32))
        s_scr[h] = cdec[h] * s_old + lax.dot_general(kd[:, hk], vh, _TN, preferred_element_type=F32)
        ret_ref[:, hv] = _gated_norm(o, rg_ref[:, hv].astype(F32)).astype(BF16)

    @pl.when(c == pl.num_programs(1) - 1)
    def _():
        sout_ref[0] = s_scr[...]


def _ret_prompt(proj, b, t):
    c = R_CHUNK
    nc = t // c
    intra, qdec, kdec, cdec = _ret_consts(c, 1)
    row = lambda bi, ci: bi * nc + ci
    return pl.pallas_call(
        functools.partial(_ret_prompt_kernel, cdec=cdec),
        grid=(b, nc),
        in_specs=[pl.BlockSpec((c, R_QK), lambda bi, ci: (row(bi, ci), OFF_RQ // R_QK)),
                  pl.BlockSpec((c, R_QK), lambda bi, ci: (row(bi, ci), OFF_RK // R_QK)),
                  pl.BlockSpec((c, R_V), lambda bi, ci: (row(bi, ci), OFF_RV // R_V)),
                  pl.BlockSpec((c, R_V), lambda bi, ci: (row(bi, ci), OFF_RG // R_V)),
                  pl.BlockSpec((R_HEADS, c, c), lambda bi, ci: (0, 0, 0)),
                  pl.BlockSpec((c, R_QK), lambda bi, ci: (0, 0)),
                  pl.BlockSpec((c, R_QK), lambda bi, ci: (0, 0))],
        out_specs=[pl.BlockSpec((c, R_V), lambda bi, ci: (row(bi, ci), 0)),
                   pl.BlockSpec((1, R_HEADS, R_DK, R_DV), lambda bi, ci: (bi, 0, 0, 0))],
        out_shape=[jax.ShapeDtypeStruct((b * t, R_V), BF16),
                   jax.ShapeDtypeStruct((b, R_HEADS, R_DK, R_DV), F32)],
        scratch_shapes=[pltpu.VMEM((R_HEADS, R_DK, R_DV), F32)],
        compiler_params=_cparams("parallel", "arbitrary"),
        name="ret_prompt",
    )(proj, proj, proj, proj, intra, qdec, kdec)


def _ret_sample_kernel(q_ref, k_ref, v_ref, rg_ref, s_ref, intra_ref, qdec_ref, kdec_ref,
                       ret_ref, sout_ref, *, cdec, tn, nb):
    q = q_ref[...]
    kf = k_ref[...].astype(F32)
    qd = (q.astype(F32) * qdec_ref[...]).astype(BF16)
    kdf = kf * kdec_ref[...]
    ks = (kf * (R_DK ** -0.5)).astype(BF16)
    rows = lax.broadcasted_iota(jnp.int32, (nb * tn, 1), 0)
    for h in range(R_HEADS):
        hk = slice(h * R_DK, (h + 1) * R_DK)
        hv = slice(h * R_DV, (h + 1) * R_DV)
        vh = v_ref[:, hv]
        sc = lax.dot_general(q[:, hk], ks[:, hk], _NT, preferred_element_type=F32) * intra_ref[h]
        o = jnp.dot(sc.astype(BF16), vh, preferred_element_type=F32)
        for s in range(nb):
            mine = (rows >= s * tn) & (rows < (s + 1) * tn)
            s_old = s_ref[s, h]
            o = o + jnp.where(mine, jnp.dot(qd[:, hk], s_old.astype(BF16), preferred_element_type=F32), 0.0)
            kd_s = jnp.where(mine, kdf[:, hk], 0.0).astype(BF16)
            sout_ref[s, h] = cdec[h] * s_old + lax.dot_general(kd_s, vh, _TN, preferred_element_type=F32)
        ret_ref[:, hv] = _gated_norm(o, rg_ref[:, hv].astype(F32)).astype(BF16)


def _ret_sample(proj, state, tn):
    db = state.shape[0]
    nb = min(db, 8)
    n = nb * tn
    intra, qdec, kdec, cdec = _ret_consts(tn, nb)
    return pl.pallas_call(
        functools.partial(_ret_sample_kernel, cdec=cdec, tn=tn, nb=nb),
        grid=(db // nb,),
        in_specs=[pl.BlockSpec((n, R_QK), lambda i: (i, OFF_RQ // R_QK)),
                  pl.BlockSpec((n, R_QK), lambda i: (i, OFF_RK // R_QK)),
                  pl.BlockSpec((n, R_V), lambda i: (i, OFF_RV // R_V)),
                  pl.BlockSpec((n, R_V), lambda i: (i, OFF_RG // R_V)),
                  pl.BlockSpec((nb, R_HEADS, R_DK, R_DV), lambda i: (i, 0, 0, 0)),
                  pl.BlockSpec((R_HEADS, n, n), lambda i: (0, 0, 0)),
                  pl.BlockSpec((n, R_QK), lambda i: (0, 0)),
                  pl.BlockSpec((n, R_QK), lambda i: (0, 0))],
        out_specs=[pl.BlockSpec((n, R_V), lambda i: (i, 0)),
                   pl.BlockSpec((nb, R_HEADS, R_DK, R_DV), lambda i: (i, 0, 0, 0))],
        out_shape=[jax.ShapeDtypeStruct((db * tn, R_V), BF16),
                   jax.ShapeDtypeStruct(state.shape, F32)],
        compiler_params=_cparams("parallel"),
        name="ret_sample",
    )(proj, proj, proj, proj, state, intra, qdec, kdec)


def _subln(o, g, lam_init):
    ms = jnp.mean(o * o, axis=-1, keepdims=True)
    return (o * lax.rsqrt(ms + SUBLN_EPS)) * g * (1.0 - lam_init)


def _dattn_prompt_kernel(sc_ref, q_ref, k_ref, v_ref, g_ref, rc_ref, o_ref,
                         qq_scr, bias_scr, m_scr, l_scr, acc_scr, *, t, lam_init):
    h = pl.program_id(1)
    qi = pl.program_id(2)
    slope = sc_ref[h]
    lam = sc_ref[A_HEADS]

    q = q_ref[...].astype(F32) * (A_DH ** -0.5)
    lane = lax.broadcasted_iota(jnp.int32, q.shape, 1)
    qq_scr[0:t, :] = jnp.where(lane < A_DH, q, 0.0).astype(BF16)
    qq_scr[t:2 * t, :] = jnp.where(lane >= A_DH, q, 0.0).astype(BF16)
    rc = rc_ref[...]
    bias_scr[0:t, :] = -slope * rc
    bias_scr[t:2 * t, :] = -slope * rc
    m_scr[...] = jnp.full_like(m_scr, NEG_BIG)
    l_scr[...] = jnp.zeros_like(l_scr)
    acc_scr[...] = jnp.zeros_like(acc_scr)

    def tile(kj, off, masked):
        kt = k_ref[pl.ds(pl.multiple_of(kj * t, t), t), :]
        vt = v_ref[pl.ds(pl.multiple_of(kj * t, t), t), :]
        s = lax.dot_general(qq_scr[...], kt, _NT, preferred_element_type=F32) + bias_scr[...]
        if masked:
            rc2 = jnp.concatenate([rc, rc], axis=0)
            s = jnp.where(rc2 >= 0.0, s, NEG_BIG)
        m_prev = m_scr[...]
        m_new = jnp.maximum(m_prev, jnp.max(s, axis=-1, keepdims=True) - off)
        alpha = jnp.exp(m_prev - m_new)
        p = jnp.exp(s - (m_new + off))
        l_scr[...] = alpha * l_scr[...] + jnp.sum(p, axis=-1, keepdims=True)
        acc_scr[...] = alpha * acc_scr[...] + jnp.dot(p.astype(BF16), vt, preferred_element_type=F32)
        m_scr[...] = m_new

    def body(kj, carry):
        tile(kj, slope * ((qi - kj) * t).astype(F32), False)
        return carry

    lax.fori_loop(0, qi, body, 0)
    tile(qi, 0.0, True)

    o = acc_scr[...] / l_scr[...]
    out = o[0:t] - lam * o[t:2 * t]
    o_ref[...] = _subln(out, g_ref[...], lam_init).astype(BF16)


def _dattn_prompt(proj, scal, subln_g, b, t_seq, lam_init):
    t = min(ATTN_T, t_seq)
    nq = t_seq // t
    r = np.arange(t, dtype=np.float32)
    rc = jnp.asarray(r[:, None] - r[None, :])
    cq, ck, cv = OFF_AQ // A_DV, OFF_AK // A_DV, OFF_AV // A_DV
    return pl.pallas_call(
        functools.partial(_dattn_prompt_kernel, t=t, lam_init=lam_init),
        grid=(b, A_HEADS, nq),
        in_specs=[pl.BlockSpec(memory_space=pltpu.SMEM),
                  pl.BlockSpec((t, A_DV), lambda bi, h, qi: (bi * nq + qi, cq + h)),
                  pl.BlockSpec((t_seq, A_DV), lambda bi, h, qi: (bi, ck + h)),
                  pl.BlockSpec((t_seq, A_DV), lambda bi, h, qi: (bi, cv + h)),
                  pl.BlockSpec((1, A_DV), lambda bi, h, qi: (0, 0)),
                  pl.BlockSpec((t, t), lambda bi, h, qi: (0, 0))],
        out_specs=pl.BlockSpec((t, A_DV), lambda bi, h, qi: (bi * nq + qi, h)),
        out_shape=jax.ShapeDtypeStruct((b * t_seq, A_V), BF16),
        scratch_shapes=[pltpu.VMEM((2 * t, A_DV), BF16),
                        pltpu.VMEM((2 * t, t), F32),
                        pltpu.VMEM((2 * t, 1), F32),
                        pltpu.VMEM((2 * t, 1), F32),
                        pltpu.VMEM((2 * t, A_DV), F32)],
        compiler_params=_cparams("parallel", "parallel", "arbitrary"),
        name="dattn_prompt",
    )(scal, proj, proj, proj, subln_g.reshape(1, A_DV), rc)


def _pattn_col_consts(tn, past_len):
    slopes = _alibi_slopes()
    ncol = 2 * A_HEADS * tn
    slope_col = np.zeros((128,), np.float64)
    q_col = np.zeros((128,), np.float64)
    for c in range(2):
        for h in range(A_HEADS):
            for q in range(tn):
                col = (c * A_HEADS + h) * tn + q
                slope_col[col] = slopes[h]
                q_col[col] = q
    rb = slope_col[None, :] * np.arange(PAGE_SIZE, dtype=np.float64)[:, None]
    rows = np.zeros((8, 128), np.float64)
    rows[0] = -slope_col * (past_len + q_col)
    rows[1] = slope_col * PAGE_SIZE
    nb = np.full((8, 128), NEG_BIG, np.float64)
    for i in range(tn):
        ok = (np.arange(128) < ncol) & (q_col >= i)
        nb[i] = np.where(ok, -slope_col * (q_col - i), NEG_BIG)
    nb[0, ncol:] = 0.0
    return (jnp.asarray(rb, F32), jnp.asarray(rows, F32), jnp.asarray(nb, F32))


def _pattn_kernel(pt_ref, lam_ref, qbd_ref, kn_ref, vn_ref, g_ref, rb_ref, rows_ref, nb_ref, ck_ref, cv_ref,
                  o_ref, kbuf, vbuf, sem, m_scr, l_scr, acc_scr, *, npg, tn, lam_init):
    j = pl.program_id(1)
    nj = pl.num_programs(1)
    step = pl.program_id(0) * nj + j
    slot = step % 2

    def page_copies(st, sl):
        cps = []
        for p_i in range(npg):
            pg = pt_ref[st * npg + p_i]
            cps.append(pltpu.make_async_copy(ck_ref.at[pg], kbuf.at[sl, p_i], sem.at[sl]))
            cps.append(pltpu.make_async_copy(cv_ref.at[pg], vbuf.at[sl, p_i], sem.at[sl]))
        return cps

    @pl.when(step == 0)
    def _():
        for cp in page_copies(0, 0):
            cp.start()

    @pl.when(step + 1 < pl.num_programs(0) * nj)
    def _():
        for cp in page_copies(step + 1, 1 - slot):
            cp.start()

    qbd = qbd_ref[...]

    def lanes_to_rows(x):
        return jnp.transpose(jnp.broadcast_to(x, (128, 128)))

    @pl.when(j == 0)
    def _():
        nbias = nb_ref[...]
        s = jnp.dot(kn_ref[...].astype(BF16), qbd, preferred_element_type=F32)
        s = jnp.where(nbias > 0.5 * NEG_BIG, s + nbias, NEG_BIG)
        m0 = jnp.max(s, axis=0, keepdims=True)
        p = jnp.exp(s - m0)
        m_scr[...] = m0
        l_scr[...] = jnp.sum(p, axis=0, keepdims=True)
        acc_scr[...] = lax.dot_general(p.astype(BF16), vn_ref[...].astype(BF16), _TN,
                                       preferred_element_type=F32)

    for cp in page_copies(step, slot):
        cp.wait()

    rb = rb_ref[...]
    arow = rows_ref[0:1, :]
    brow = rows_ref[1:2, :]
    s_all = []
    for p_i in range(npg):
        page = (j * npg + p_i).astype(F32)
        s = jnp.dot(kbuf[slot, p_i].astype(BF16), qbd, preferred_element_type=F32)
        s_all.append(s + rb + (arow + brow * page))
    mx = s_all[0]
    for s in s_all[1:]:
        mx = jnp.maximum(mx, s)
    m_prev = m_scr[...]
    m_new = jnp.maximum(m_prev, jnp.max(mx, axis=0, keepdims=True))
    alpha = jnp.exp(m_prev - m_new)
    lsum = jnp.zeros((1, 128), F32)
    pv = jnp.zeros((128, A_V), F32)
    for p_i in range(npg):
        p = jnp.exp(s_all[p_i] - m_new)
        lsum = lsum + jnp.sum(p, axis=0, keepdims=True)
        pv = pv + lax.dot_general(p.astype(BF16), vbuf[slot, p_i].astype(BF16), _TN,
                                  preferred_element_type=F32)
    m_scr[...] = m_new
    l_scr[...] = alpha * l_scr[...] + lsum
    a_rows = lanes_to_rows(alpha)
    acc_scr[...] = jnp.concatenate([a_rows] * A_HEADS, axis=1) * acc_scr[...] + pv

    @pl.when(j == nj - 1)
    def _():
        l_rows = lanes_to_rows(l_scr[...])
        lam = lam_ref[0]
        g = g_ref[...]
        half = A_HEADS * tn
        for h in range(A_HEADS):
            cols = slice(h * A_DV, (h + 1) * A_DV)
            blk = acc_scr[:, cols] / l_rows
            out = blk[h * tn:(h + 1) * tn] - lam * blk[half + h * tn:half + (h + 1) * tn]
            o_ref[:, cols] = _subln(out, g, lam_init)


def _dattn_sample(qbd, k_new, v_new, cache_k, cache_v, page_table, lam, subln_g, tn, lam_init):
    db, n_pages = page_table.shape
    npg = min(PAGES_PER_STEP, n_pages)
    rb, rows, nb = _pattn_col_consts(tn, n_pages * PAGE_SIZE)
    const = lambda shape: pl.BlockSpec(shape, lambda bi, j, pt: (0,) * len(shape))
    grid_spec = pltpu.PrefetchScalarGridSpec(
        num_scalar_prefetch=1,
        grid=(db, n_pages // npg),
        in_specs=[pl.BlockSpec(memory_space=pltpu.SMEM),
                  pl.BlockSpec((None, A_QK, 128), lambda bi, j, pt: (bi, 0, 0)),
                  pl.BlockSpec((None, 8, A_QK), lambda bi, j, pt: (bi, 0, 0)),
                  pl.BlockSpec((None, 8, A_V), lambda bi, j, pt: (bi, 0, 0)),
                  const((1, A_DV)), const((PAGE_SIZE, 128)), const((8, 128)), const((8, 128)),
                  pl.BlockSpec(memory_space=pl.ANY), pl.BlockSpec(memory_space=pl.ANY)],
        out_specs=pl.BlockSpec((None, tn, A_V), lambda bi, j, pt: (bi, 0, 0)),
        scratch_shapes=[pltpu.VMEM((2, npg, PAGE_SIZE, A_QK), F32),
                        pltpu.VMEM((2, npg, PAGE_SIZE, A_V), F32),
                        pltpu.SemaphoreType.DMA((2,)),
                        pltpu.VMEM((1, 128), F32), pltpu.VMEM((1, 128), F32), pltpu.VMEM((128, A_V), F32)],
    )
    return pl.pallas_call(
        functools.partial(_pattn_kernel, npg=npg, tn=tn, lam_init=lam_init),
        grid_spec=grid_spec,
        out_shape=jax.ShapeDtypeStruct((db, tn, A_V), F32),
        compiler_params=_cparams("arbitrary", "arbitrary"),
        name="dattn_sample",
    )(page_table.reshape(-1), lam.reshape(1), qbd, k_new, v_new, subln_g.reshape(1, A_DV), rb, rows, nb,
      cache_k, cache_v)


ROUTER_ROWS = 8 + N_EXPERTS


def _merge_kernel(x_ref, ret_ref, dif_ref, ga0_ref, ga1_ref, gb0_ref, gb1_ref, wro_ref, wdo_ref, wo_ref, gf_ref,
                  wr_ref, br_ref, h_ref, xn_ref, re_ref, rw_ref):
    a = jnp.dot(ret_ref[...], wro_ref[...], preferred_element_type=F32)
    d = jnp.dot(dif_ref[...], wdo_ref[...], preferred_element_type=F32)
    ga = jnp.concatenate([ga0_ref[...], ga1_ref[...]], axis=1).astype(F32)
    gb = jnp.concatenate([gb0_ref[...], gb1_ref[...]], axis=1).astype(F32)
    m = jax.nn.sigmoid(ga) * a + jax.nn.sigmoid(gb) * d
    h1 = x_ref[...] + jnp.dot(m.astype(BF16), wo_ref[...], preferred_element_type=F32)
    h_ref[...] = h1
    ms = jnp.mean(h1 * h1, axis=-1, keepdims=True)
    xn = h1 * lax.rsqrt(ms + EPS) * gf_ref[...]
    xn_ref[...] = xn

    lt = lax.dot_general(wr_ref[...], xn.astype(BF16), _NT, preferred_element_type=F32) + br_ref[:, 0:1]
    bm = lt.shape[1]
    row8 = lax.broadcasted_iota(jnp.int32, (8, bm), 0)

    def first_argmax(v):
        top = jnp.max(v, axis=0, keepdims=True)
        return top, jnp.min(jnp.where(v == top, row8, 8), axis=0, keepdims=True)

    lg = jnp.where(row8 < N_GROUPS, lt[0:8], NEG_BIG)
    gmax, g_sel = first_argmax(lg)
    g_w = 1.0 / jnp.sum(jnp.exp(lg - gmax), axis=0, keepdims=True)
    el = jnp.zeros((8, bm), F32)
    for g in range(N_GROUPS):
        el = jnp.where(g_sel == g, lt[8 + 8 * g:16 + 8 * g], el)
    ee = jnp.exp(el - jnp.max(el, axis=0, keepdims=True))
    ep = ee / jnp.sum(ee, axis=0, keepdims=True)
    p1, i1 = first_argmax(ep)
    p2, i2 = first_argmax(jnp.where(row8 == i1, -1.0, ep))
    den = p1 + p2
    re_ref[0:1, :] = g_sel * EXP_PER_GROUP + i1
    re_ref[1:2, :] = g_sel * EXP_PER_GROUP + i2
    rw_ref[0:1, :] = g_w * p1 / den
    rw_ref[1:2, :] = g_w * p2 / den


def _merge(x, ret, dif, proj, w_ret_out, w_diff_out, w_out, norm_ffn, w_router, b_router):
    m = x.shape[0]
    bm = min(m, 512)
    const = lambda shape: pl.BlockSpec(shape, lambda i: (0,) * len(shape))
    half = D_MODEL // 2
    gate = lambda off: pl.BlockSpec((bm, half), lambda i: (i, off // half))
    return pl.pallas_call(
        _merge_kernel,
        grid=(m // bm,),
        in_specs=[pl.BlockSpec((bm, D_MODEL), lambda i: (i, 0)),
                  pl.BlockSpec((bm, R_V), lambda i: (i, 0)),
                  pl.BlockSpec((bm, A_V), lambda i: (i, 0)),
                  gate(OFF_GA), gate(OFF_GA + half), gate(OFF_GB), gate(OFF_GB + half),
                  const((R_V, D_MODEL)), const((A_V, D_MODEL)), const((D_MODEL, D_MODEL)),
                  const((1, D_MODEL)), const((ROUTER_ROWS, D_MODEL)), const((ROUTER_ROWS, 128))],
        out_specs=[pl.BlockSpec((bm, D_MODEL), lambda i: (i, 0)),
                   pl.BlockSpec((bm, D_MODEL), lambda i: (i, 0)),
                   pl.BlockSpec((2, bm), lambda i: (0, i)),
                   pl.BlockSpec((2, bm), lambda i: (0, i))],
        out_shape=[jax.ShapeDtypeStruct((m, D_MODEL), F32),
                   jax.ShapeDtypeStruct((m, D_MODEL), F32),
                   jax.ShapeDtypeStruct((2, m), jnp.int32),
                   jax.ShapeDtypeStruct((2, m), F32)],
        compiler_params=_cparams("parallel"),
        name="merge",
    )(x, ret, dif, proj, proj, proj, proj, w_ret_out, w_diff_out, w_out, norm_ffn.reshape(1, D_MODEL),
      w_router, b_router)


def _plan_kernel(re_ref, tri_ref, rank_ref, cnt_ref, carry_scr):
    @pl.when(pl.program_id(0) == 0)
    def _():
        carry_scr[...] = jnp.zeros_like(carry_scr)

    tb = re_ref.shape[1]
    erow = lax.broadcasted_iota(jnp.int32, (N_EXPERTS, tb), 0)
    oh0 = jnp.where(erow == re_ref[0:1, :], 1.0, 0.0)
    oh1 = jnp.where(erow == re_ref[1:2, :], 1.0, 0.0)
    both = oh0 + oh1
    before = jnp.dot(both.astype(BF16), tri_ref[...], preferred_element_type=F32) + carry_scr[:, 0:1]
    rank_ref[0:1, :] = jnp.sum(oh0 * before, axis=0, keepdims=True).astype(jnp.int32)
    rank_ref[1:2, :] = jnp.sum(oh1 * before, axis=0, keepdims=True).astype(jnp.int32)
    carry_scr[...] = carry_scr[...] + jnp.sum(both, axis=1, keepdims=True)
    cnt_ref[...] = carry_scr[...]


def _plan(expert_id):
    m = expert_id.shape[1]
    tb = min(m, 512)
    tri = jnp.asarray(np.triu(np.ones((tb, tb), np.float32), 1), BF16)
    return pl.pallas_call(
        _plan_kernel,
        grid=(m // tb,),
        in_specs=[pl.BlockSpec((2, tb), lambda i: (0, i)),
                  pl.BlockSpec((tb, tb), lambda i: (0, 0))],
        out_specs=[pl.BlockSpec((2, tb), lambda i: (0, i)),
                   pl.BlockSpec((N_EXPERTS, 128), lambda i: (0, 0))],
        out_shape=[jax.ShapeDtypeStruct((2, m), jnp.int32),
                   jax.ShapeDtypeStruct((N_EXPERTS, 128), F32)],
        scratch_shapes=[pltpu.VMEM((N_EXPERTS, 128), F32)],
        compiler_params=_cparams("arbitrary"),
        name="moe_plan",
    )(expert_id, tri)


def _dispatch_kernel(dest_ref, pad_ref, nu_ref, x_ref, xs_ref, zero_scr, sem, zsem):
    tb = x_ref.shape[0]

    def row_copy(t, s):
        return pltpu.make_async_copy(x_ref.at[pl.ds(t, 1)], xs_ref.at[pl.ds(dest_ref[s, t], 1)], sem)

    def issue(t, carry):
        row_copy(t, 0).start()
        row_copy(t, 1).start()
        return carry

    lax.fori_loop(0, tb, issue, 0)

    @pl.when(pl.program_id(0) == 0)
    def _():
        zero_scr[...] = jnp.zeros_like(zero_scr)

        def pad_expert(e, carry):
            start, n = pad_ref[0, e], pad_ref[1, e]
            zero_row = lambda r: pltpu.make_async_copy(
                zero_scr.at[pl.ds(0, 1)], xs_ref.at[pl.ds(start + r, 1)], zsem)
            lax.fori_loop(0, n, lambda r, c: (zero_row(r).start(), c)[1], 0)
            lax.fori_loop(0, n, lambda r, c: (zero_row(r).wait(), c)[1], 0)
            return carry

        lax.fori_loop(0, N_EXPERTS, pad_expert, 0)
        nblk = xs_ref.shape[0] // MOE_TM
        zero_blk = lambda i: pltpu.make_async_copy(
            zero_scr, xs_ref.at[pl.ds(pl.multiple_of(i * MOE_TM, MOE_TM), MOE_TM)], zsem)
        lax.fori_loop(nu_ref[0], nblk, lambda i, c: (zero_blk(i).start(), c)[1], 0)
        lax.fori_loop(nu_ref[0], nblk, lambda i, c: (zero_blk(i).wait(), c)[1], 0)

    def drain(t, carry):
        row_copy(t, 0).wait()
        row_copy(t, 1).wait()
        return carry

    lax.fori_loop(0, tb, drain, 0)


def _dispatch(xn, dest, pad_info, n_used, rows):
    m = xn.shape[0]
    tb = min(m, 512)
    return pl.pallas_call(
        _dispatch_kernel,
        grid=(m // tb,),
        in_specs=[pl.BlockSpec((2, tb), lambda i: (0, i), memory_space=pltpu.SMEM),
                  pl.BlockSpec(memory_space=pltpu.SMEM),
                  pl.BlockSpec(memory_space=pltpu.SMEM),
                  pl.BlockSpec((tb, D_MODEL), lambda i: (i, 0))],
        out_specs=pl.BlockSpec(memory_space=pl.ANY),
        out_shape=jax.ShapeDtypeStruct((rows, D_MODEL), F32),
        scratch_shapes=[pltpu.VMEM((MOE_TM, D_MODEL), F32),
                        pltpu.SemaphoreType.DMA(()), pltpu.SemaphoreType.DMA(())],
        compiler_params=_cparams("arbitrary"),
        name="moe_dispatch",
    )(dest, pad_info, n_used, xn)


def _expert_kernel(be_ref, nu_ref, x_ref, wg_ref, wu_ref, wd_ref, y_ref):
    live = pl.program_id(0) < nu_ref[0]

    @pl.when(jnp.logical_not(live))
    def _():
        y_ref[...] = jnp.zeros_like(y_ref)

    @pl.when(live)
    def _():
        x = x_ref[...].astype(BF16)
        gate = jnp.dot(x, wg_ref[...], preferred_element_type=F32)
        up = jnp.dot(x, wu_ref[...], preferred_element_type=F32)
        hid = (gate * jax.nn.sigmoid(gate)) * up
        y_ref[...] = jnp.dot(hid.astype(BF16), wd_ref[...], preferred_element_type=F32)


def _experts(xs, block_e, n_used, w_gate, w_up, w_down):
    rows = xs.shape[0]
    nblk = rows // MOE_TM
    live = lambda i, be, nu: jnp.minimum(i, nu[0] - 1)
    grid_spec = pltpu.PrefetchScalarGridSpec(
        num_scalar_prefetch=2,
        grid=(nblk,),
        in_specs=[pl.BlockSpec((MOE_TM, D_MODEL), lambda i, be, nu: (live(i, be, nu), 0)),
                  pl.BlockSpec((None, D_MODEL, D_EXPERT), lambda i, be, nu: (be[live(i, be, nu)], 0, 0)),
                  pl.BlockSpec((None, D_MODEL, D_EXPERT), lambda i, be, nu: (be[live(i, be, nu)], 0, 0)),
                  pl.BlockSpec((None, D_EXPERT, D_MODEL), lambda i, be, nu: (be[live(i, be, nu)], 0, 0))],
        out_specs=pl.BlockSpec((MOE_TM, D_MODEL), lambda i, be, nu: (i, 0)),
    )
    return pl.pallas_call(
        _expert_kernel,
        grid_spec=grid_spec,
        out_shape=jax.ShapeDtypeStruct((rows, D_MODEL), F32),
        compiler_params=_cparams("arbitrary"),
        name="moe_experts",
    )(block_e, n_used, xs, w_gate, w_up, w_down)


def _combine_kernel(dest_ref, h_ref, w_ref, gf_ref, yb_ref, y_ref, g_scr, sem):
    tb = h_ref.shape[0]

    def row_copy(t, s):
        return pltpu.make_async_copy(yb_ref.at[pl.ds(dest_ref[s, t], 1)], g_scr.at[s, pl.ds(t, 1)], sem)

    def issue(t, carry):
        row_copy(t, 0).start()
        row_copy(t, 1).start()
        return carry

    def drain(t, carry):
        row_copy(t, 0).wait()
        row_copy(t, 1).wait()
        return carry

    lax.fori_loop(0, tb, issue, 0)
    lax.fori_loop(0, tb, drain, 0)
    y = h_ref[...] + (w_ref[:, 0:1] * g_scr[0] + w_ref[:, 1:2] * g_scr[1])
    ms = jnp.mean(y * y, axis=-1, keepdims=True)
    y_ref[...] = y * lax.rsqrt(ms + EPS) * gf_ref[...]


def _combine(h1, yb, dest, weights_t, norm_final):
    m = h1.shape[0]
    tb = min(m, 256)
    return pl.pallas_call(
        _combine_kernel,
        grid=(m // tb,),
        in_specs=[pl.BlockSpec((2, tb), lambda i: (0, i), memory_space=pltpu.SMEM),
                  pl.BlockSpec((tb, D_MODEL), lambda i: (i, 0)),
                  pl.BlockSpec((tb, 2), lambda i: (i, 0)),
                  pl.BlockSpec((1, D_MODEL), lambda i: (0, 0)),
                  pl.BlockSpec(memory_space=pl.ANY)],
        out_specs=pl.BlockSpec((tb, D_MODEL), lambda i: (i, 0)),
        out_shape=jax.ShapeDtypeStruct((m, D_MODEL), F32),
        scratch_shapes=[pltpu.VMEM((2, tb, D_MODEL), F32), pltpu.SemaphoreType.DMA(())],
        compiler_params=_cparams("arbitrary"),
        name="moe_combine",
    )(dest, h1, weights_t, norm_final.reshape(1, D_MODEL), yb)


def _moe_and_final(h1, xn, expert_id, weights, w_gate, w_up, w_down, norm_final):
    m = h1.shape[0]
    rank, counts = _plan(expert_id)
    counts = counts[:, 0].astype(jnp.int32)
    padded = (counts + MOE_TM - 1) // MOE_TM * MOE_TM
    pad_end = jnp.cumsum(padded)
    pad_start = pad_end - padded
    dest = pad_start[expert_id] + rank
    nblk = (m * TOP_K) // MOE_TM + N_EXPERTS
    block_start = jnp.arange(nblk, dtype=jnp.int32) * MOE_TM
    block_e = jnp.minimum(jnp.sum((block_start[:, None] >= pad_end[None, :]).astype(jnp.int32), axis=1),
                          N_EXPERTS - 1).astype(jnp.int32)
    n_used = (pad_end[-1:] // MOE_TM).astype(jnp.int32)
    pad_info = jnp.stack([pad_start + counts, padded - counts]).astype(jnp.int32)
    xs = _dispatch(xn, dest, pad_info, n_used, nblk * MOE_TM)
    yb = _experts(xs, block_e, n_used, w_gate, w_up, w_down)
    return _combine(h1, yb, dest, weights.T, norm_final)


def kernel(x_prompt, x_sample, state_ret, cache_k, cache_v, page_table, norm_mix, w_in, lambda_q1, lambda_k1,
           lambda_q2, lambda_k2, subln_g, w_ret_out, w_diff_out, w_out, norm_ffn, w_group, b_group, w_expert,
           b_expert, w_up, w_gate, w_down, norm_final):
    b, t_seq, _ = x_prompt.shape
    db, tn, _ = x_sample.shape
    depth = w_in.shape[0]
    assert depth == 1 and tn <= 8
    n_phys = cache_k.shape[1]
    l = 0
    lam_init = 0.8 - 0.6 * float(np.exp(-0.3 * l))
    lam = (jnp.exp(jnp.sum(lambda_q1[l].astype(F32) * lambda_k1[l].astype(F32)))
           - jnp.exp(jnp.sum(lambda_q2[l].astype(F32) * lambda_k2[l].astype(F32))) + lam_init)
    scal = jnp.concatenate([jnp.asarray(_alibi_slopes(), F32), lam.reshape(1)])

    w_in_b = w_in[l].astype(BF16)
    wro, wdo, wo = w_ret_out[l].astype(BF16), w_diff_out[l].astype(BF16), w_out[l].astype(BF16)
    wg_b, wu_b, wd_b = w_gate[l].astype(BF16), w_up[l].astype(BF16), w_down[l].astype(BF16)
    w_router = jnp.zeros((ROUTER_ROWS, D_MODEL), F32)
    w_router = w_router.at[0:N_GROUPS].set(w_group[l].T).at[8:].set(w_expert[l].T).astype(BF16)
    b_router = jnp.zeros((ROUTER_ROWS,), F32).at[0:N_GROUPS].set(b_group[l]).at[8:].set(b_expert[l])
    b_router = jnp.broadcast_to(b_router[:, None], (ROUTER_ROWS, 128))

    def tail(x, ret, dif, proj):
        h1, xn, eid, wts = _merge(x, ret, dif, proj, wro, wdo, wo, norm_ffn[l], w_router, b_router)
        return _moe_and_final(h1, xn, eid, wts, wg_b, wu_b, wd_b, norm_final)

    xp = x_prompt.reshape(b * t_seq, D_MODEL)
    proj_p, k_p, v_p = _proj(xp, norm_mix[l], w_in_b)
    ret_p, s_p = _ret_prompt(proj_p, b, t_seq)
    dif_p = _dattn_prompt(proj_p, scal, subln_g[l], b, t_seq, lam_init)
    y_p = tail(xp, ret_p, dif_p, proj_p)

    xs = x_sample.reshape(db * tn, D_MODEL)
    proj_s, k_s, v_s = _proj(xs, norm_mix[l], w_in_b)
    ret_s, s_s = _ret_sample(proj_s, state_ret[l], tn)
    aq = proj_s[:, OFF_AQ:OFF_AQ + A_QK].astype(F32).reshape(db, tn, A_HEADS, 2, A_DH) * (A_DH ** -0.5)
    eye_h = jnp.eye(A_HEADS, dtype=F32)
    eye_c = jnp.eye(2, dtype=F32)
    qbd = jnp.einsum('bqhcd,hH,cC->bhcdCHq', aq, eye_h, eye_c).reshape(db, A_QK, 2 * A_HEADS * tn)
    qbd = jnp.pad(qbd, ((0, 0), (0, 0), (0, 128 - 2 * A_HEADS * tn))).astype(BF16)
    pad_new = lambda a: jnp.pad(a.reshape(db, tn, -1), ((0, 0), (0, 8 - tn), (0, 0)))
    dif_s = _dattn_sample(qbd, pad_new(k_s), pad_new(v_s),
                          cache_k[l].reshape(n_phys, PAGE_SIZE, A_QK), cache_v[l].reshape(n_phys, PAGE_SIZE, A_V),
                          page_table.astype(jnp.int32), lam, subln_g[l], tn, lam_init)
    y_s = tail(xs, ret_s, dif_s.reshape(db * tn, A_V).astype(BF16), proj_s)

    return (y_p.reshape(b, t_seq, D_MODEL),
            y_s.reshape(db, tn, D_MODEL),
            s_p[None],
            s_s[None],
            k_p.reshape(1, b, t_seq, A_HEADS, 2, A_DH),
            v_p.reshape(1, b, t_seq, A_HEADS, A_DV),
            k_s.reshape(1, db, tn, A_HEADS, 2, A_DH),
            v_s.reshape(1, db, tn, A_HEADS, A_DV))
```

```python
import functools

import numpy as np
import jax
import jax.numpy as jnp
from jax import lax
from jax.experimental import pallas as pl
from jax.experimental.pallas import tpu as pltpu

D_MODEL = 1024
R_HEADS, R_DK, R_DV, R_CHUNK = 4, 128, 256, 128
A_HEADS, A_DH, A_DV = 4, 64, 128
N_GROUPS, EXP_PER_GROUP, TOP_K, D_EXPERT = 4, 8, 2, 512
N_EXPERTS = N_GROUPS * EXP_PER_GROUP
PAGE_SIZE = 128
EPS, SUBLN_EPS, NEG_BIG = 1e-6, 1e-5, -1e30

R_QK, R_V = R_HEADS * R_DK, R_HEADS * R_DV
A_QK, A_V = A_HEADS * 2 * A_DH, A_HEADS * A_DV
OFF_RQ, OFF_RK, OFF_RV, OFF_RG = 0, R_QK, 2 * R_QK, 2 * R_QK + R_V
OFF_AQ = OFF_RG + R_V
OFF_AK, OFF_AV = OFF_AQ + A_QK, OFF_AQ + 2 * A_QK
OFF_GA = OFF_AV + A_V
OFF_GB = OFF_GA + D_MODEL
N_IN = OFF_GB + D_MODEL

PROJ_BN = 512
ATTN_T = 256
PAGES_PER_STEP = 16
MOE_TM = 256
VMEM_LIMIT = 56 * 1024 * 1024

F32, BF16 = jnp.float32, jnp.bfloat16
_NT = (((1,), (1,)), ((), ()))
_TN = (((0,), (0,)), ((), ()))


def _cparams(*sem):
    return pltpu.CompilerParams(dimension_semantics=sem, vmem_limit_bytes=VMEM_LIMIT)


def _log_decay():
    return np.log1p(-np.exp2(-5.0 - np.arange(R_HEADS, dtype=np.float64)))


def _alibi_slopes():
    return np.exp2(-8.0 * np.arange(1, A_HEADS + 1, dtype=np.float64) / A_HEADS)


def _proj_kernel(x_ref, g_ref, w_ref, wkt_ref, p_ref, k_ref, v_ref, xn_ref, *, seq_major):
    j = pl.program_id(1)

    @pl.when(j == 0)
    def _():
        x = x_ref[...]
        ms = jnp.mean(x * x, axis=-1, keepdims=True)
        xn_ref[...] = (x * lax.rsqrt(ms + EPS) * g_ref[...]).astype(BF16)

    acc = jnp.dot(xn_ref[...], w_ref[...], preferred_element_type=F32)
    p_ref[...] = acc.astype(BF16)

    @pl.when(j == OFF_AK // PROJ_BN)
    def _():
        if seq_major:
            k_ref[...] = lax.dot_general(wkt_ref[...], xn_ref[...], _NT, preferred_element_type=F32)
        else:
            k_ref[...] = acc

    @pl.when(j == OFF_AV // PROJ_BN)
    def _():
        if seq_major:
            for h in range(A_HEADS):
                v_ref[pl.ds(h, acc.shape[0], stride=A_HEADS), :] = acc[:, h * A_DV:(h + 1) * A_DV]
        else:
            v_ref[...] = acc


def _proj(x, gain, w_bf16, wkt_bf16, seq_len=None):
    m = x.shape[0]
    bm = min(m, 1024) if seq_len is None else min(seq_len, 1024)
    if seq_len is None:
        kv_specs = [pl.BlockSpec((bm, A_QK), lambda i, j: (i, 0)),
                    pl.BlockSpec((bm, A_V), lambda i, j: (i, 0))]
        kv_shapes = [jax.ShapeDtypeStruct((m, A_QK), F32), jax.ShapeDtypeStruct((m, A_V), F32)]
    else:
        per_seq = seq_len // bm
        kv_specs = [pl.BlockSpec((None, A_QK, bm), lambda i, j: (i // per_seq, 0, i % per_seq)),
                    pl.BlockSpec((bm * A_HEADS, A_DV), lambda i, j: (i, 0))]
        kv_shapes = [jax.ShapeDtypeStruct((m // seq_len, A_QK, seq_len), F32),
                     jax.ShapeDtypeStruct((m * A_HEADS, A_DV), F32)]
    return pl.pallas_call(
        functools.partial(_proj_kernel, seq_major=seq_len is not None),
        grid=(m // bm, N_IN // PROJ_BN),
        in_specs=[pl.BlockSpec((bm, D_MODEL), lambda i, j: (i, 0)),
                  pl.BlockSpec((1, D_MODEL), lambda i, j: (0, 0)),
                  pl.BlockSpec((D_MODEL, PROJ_BN), lambda i, j: (0, j)),
                  pl.BlockSpec((A_QK, D_MODEL), lambda i, j: (0, 0))],
        out_specs=[pl.BlockSpec((bm, PROJ_BN), lambda i, j: (i, j))] + kv_specs,
        out_shape=[jax.ShapeDtypeStruct((m, N_IN), BF16)] + kv_shapes,
        scratch_shapes=[pltpu.VMEM((bm, D_MODEL), BF16)],
        compiler_params=_cparams("parallel", "arbitrary"),
        name="proj",
    )(x, gain.reshape(1, D_MODEL), w_bf16, wkt_bf16)


def _ret_consts(c, rows_per_seq):
    lg = _log_decay()
    pos = np.arange(c, dtype=np.float64)
    dist = pos[:, None] - pos[None, :]
    n = c * rows_per_seq
    intra = np.zeros((R_HEADS, n, n), np.float32)
    qdec = np.zeros((n, R_QK), np.float32)
    kdec = np.zeros((n, R_QK), np.float32)
    for h in range(R_HEADS):
        blk = np.where(dist >= 0, np.exp(np.maximum(dist, 0.0) * lg[h]), 0.0)
        for s in range(rows_per_seq):
            intra[h, s * c:(s + 1) * c, s * c:(s + 1) * c] = blk
            qdec[s * c:(s + 1) * c, h * R_DK:(h + 1) * R_DK] = np.exp((pos + 1.0) * lg[h])[:, None]
            kdec[s * c:(s + 1) * c, h * R_DK:(h + 1) * R_DK] = (
                np.exp((c - 1.0 - pos) * lg[h])[:, None] * R_DK ** -0.5)
    cdec = [float(np.float32(np.exp(c * lg[h]))) for h in range(R_HEADS)]
    return jnp.asarray(intra), jnp.asarray(qdec), jnp.asarray(kdec), cdec


def _gated_norm(o, g):
    ms = jnp.mean(o * o, axis=-1, keepdims=True)
    return (o * lax.rsqrt(ms + EPS)) * (g * jax.nn.sigmoid(g))


def _ret_prompt_kernel(q_ref, k_ref, v_ref, rg_ref, intra_ref, qdec_ref, kdec_ref,
                       ret_ref, sout_ref, s_scr, *, cdec):
    c = pl.program_id(1)

    @pl.when(c == 0)
    def _():
        s_scr[...] = jnp.zeros_like(s_scr)

    q = q_ref[...]
    kf = k_ref[...].astype(F32)
    qd = (q.astype(F32) * qdec_ref[...]).astype(BF16)
    kd = (kf * kdec_ref[...]).astype(BF16)
    ks = (kf * (R_DK ** -0.5)).astype(BF16)
    for h in range(R_HEADS):
        hk = slice(h * R_DK, (h + 1) * R_DK)
        hv = slice(h * R_DV, (h + 1) * R_DV)
        vh = v_ref[:, hv]
        s_old = s_scr[h]
        sc = lax.dot_general(q[:, hk], ks[:, hk], _NT, preferred_element_type=F32) * intra_ref[h]
        o = (jnp.dot(sc.astype(BF16), vh, preferred_element_type=F32)
             + jnp.dot(qd[:, hk], s_old.astype(BF16), preferred_element_type=F32))
        s_scr[h] = cdec[h] * s_old + lax.dot_general(kd[:, hk], vh, _TN, preferred_element_type=F32)
        ret_ref[:, hv] = _gated_norm(o, rg_ref[:, hv].astype(F32)).astype(BF16)

    @pl.when(c == pl.num_programs(1) - 1)
    def _():
        sout_ref[0] = s_scr[...]


def _ret_prompt(proj, b, t):
    c = R_CHUNK
    nc = t // c
    intra, qdec, kdec, cdec = _ret_consts(c, 1)
    row = lambda bi, ci: bi * nc + ci
    return pl.pallas_call(
        functools.partial(_ret_prompt_kernel, cdec=cdec),
        grid=(b, nc),
        in_specs=[pl.BlockSpec((c, R_QK), lambda bi, ci: (row(bi, ci), OFF_RQ // R_QK)),
                  pl.BlockSpec((c, R_QK), lambda bi, ci: (row(bi, ci), OFF_RK // R_QK)),
                  pl.BlockSpec((c, R_V), lambda bi, ci: (row(bi, ci), OFF_RV // R_V)),
                  pl.BlockSpec((c, R_V), lambda bi, ci: (row(bi, ci), OFF_RG // R_V)),
                  pl.BlockSpec((R_HEADS, c, c), lambda bi, ci: (0, 0, 0)),
                  pl.BlockSpec((c, R_QK), lambda bi, ci: (0, 0)),
                  pl.BlockSpec((c, R_QK), lambda bi, ci: (0, 0))],
        out_specs=[pl.BlockSpec((c, R_V), lambda bi, ci: (row(bi, ci), 0)),
                   pl.BlockSpec((1, R_HEADS, R_DK, R_DV), lambda bi, ci: (bi, 0, 0, 0))],
        out_shape=[jax.ShapeDtypeStruct((b * t, R_V), BF16),
                   jax.ShapeDtypeStruct((b, R_HEADS, R_DK, R_DV), F32)],
        scratch_shapes=[pltpu.VMEM((R_HEADS, R_DK, R_DV), F32)],
        compiler_params=_cparams("parallel", "arbitrary"),
        name="ret_prompt",
    )(proj, proj, proj, proj, intra, qdec, kdec)


def _ret_sample_kernel(q_ref, k_ref, v_ref, rg_ref, s_ref, intra_ref, qdec_ref, kdec_ref,
                       ret_ref, sout_ref, *, cdec, tn, nb):
    q = q_ref[...]
    kf = k_ref[...].astype(F32)
    qd = (q.astype(F32) * qdec_ref[...]).astype(BF16)
    kdf = kf * kdec_ref[...]
    ks = (kf * (R_DK ** -0.5)).astype(BF16)
    rows = lax.broadcasted_iota(jnp.int32, (nb * tn, 1), 0)
    for h in range(R_HEADS):
        hk = slice(h * R_DK, (h + 1) * R_DK)
        hv = slice(h * R_DV, (h + 1) * R_DV)
        vh = v_ref[:, hv]
        sc = lax.dot_general(q[:, hk], ks[:, hk], _NT, preferred_element_type=F32) * intra_ref[h]
        o = jnp.dot(sc.astype(BF16), vh, preferred_element_type=F32)
        for s in range(nb):
            mine = (rows >= s * tn) & (rows < (s + 1) * tn)
            s_old = s_ref[s, h]
            o = o + jnp.where(mine, jnp.dot(qd[:, hk], s_old.astype(BF16), preferred_element_type=F32), 0.0)
            kd_s = jnp.where(mine, kdf[:, hk], 0.0).astype(BF16)
            sout_ref[s, h] = cdec[h] * s_old + lax.dot_general(kd_s, vh, _TN, preferred_element_type=F32)
        ret_ref[:, hv] = _gated_norm(o, rg_ref[:, hv].astype(F32)).astype(BF16)


def _ret_sample(proj, state, tn):
    db = state.shape[0]
    nb = min(db, 8)
    n = nb * tn
    intra, qdec, kdec, cdec = _ret_consts(tn, nb)
    return pl.pallas_call(
        functools.partial(_ret_sample_kernel, cdec=cdec, tn=tn, nb=nb),
        grid=(db // nb,),
        in_specs=[pl.BlockSpec((n, R_QK), lambda i: (i, OFF_RQ // R_QK)),
                  pl.BlockSpec((n, R_QK), lambda i: (i, OFF_RK // R_QK)),
                  pl.BlockSpec((n, R_V), lambda i: (i, OFF_RV // R_V)),
                  pl.BlockSpec((n, R_V), lambda i: (i, OFF_RG // R_V)),
                  pl.BlockSpec((nb, R_HEADS, R_DK, R_DV), lambda i: (i, 0, 0, 0)),
                  pl.BlockSpec((R_HEADS, n, n), lambda i: (0, 0, 0)),
                  pl.BlockSpec((n, R_QK), lambda i: (0, 0)),
                  pl.BlockSpec((n, R_QK), lambda i: (0, 0))],
        out_specs=[pl.BlockSpec((n, R_V), lambda i: (i, 0)),
                   pl.BlockSpec((nb, R_HEADS, R_DK, R_DV), lambda i: (i, 0, 0, 0))],
        out_shape=[jax.ShapeDtypeStruct((db * tn, R_V), BF16),
                   jax.ShapeDtypeStruct(state.shape, F32)],
        compiler_params=_cparams("parallel"),
        name="ret_sample",
    )(proj, proj, proj, proj, state, intra, qdec, kdec)


def _subln(o, g, lam_init):
    ms = jnp.mean(o * o, axis=-1, keepdims=True)
    return (o * lax.rsqrt(ms + SUBLN_EPS)) * g * (1.0 - lam_init)


def _dattn_prompt_kernel(lam_ref, q_ref, k_ref, v_ref, gt_ref, rc_ref, o_ref,
                         qq_scr, bias_scr, vt_scr, m_scr, l_scr, acc_scr, *, t, lam_init, slopes):
    qi = pl.program_id(1)
    lam = lam_ref[0]
    heads = [slice(h * A_DV, (h + 1) * A_DV) for h in range(A_HEADS)]

    @pl.when(qi == 0)
    def _():
        for h in range(A_HEADS):
            for kj in range(v_ref.shape[0] // t):
                vt_scr[h, kj] = jnp.transpose(v_ref[kj * t:(kj + 1) * t, heads[h]].astype(F32)).astype(BF16)

    rc = rc_ref[...]
    for h in range(A_HEADS):
        qt = jnp.transpose(q_ref[:, heads[h]].astype(F32) * (A_DH ** -0.5))
        feat = lax.broadcasted_iota(jnp.int32, qt.shape, 0)
        qq_scr[h, :, 0:t] = jnp.where(feat < A_DH, qt, 0.0).astype(BF16)
        qq_scr[h, :, t:2 * t] = jnp.where(feat >= A_DH, qt, 0.0).astype(BF16)
        bias_scr[h, :, 0:t] = -slopes[h] * rc
        bias_scr[h, :, t:2 * t] = -slopes[h] * rc
    m_scr[...] = jnp.full_like(m_scr, NEG_BIG)
    l_scr[...] = jnp.zeros_like(l_scr)
    acc_scr[...] = jnp.zeros_like(acc_scr)

    def tile(kj, dist, masked):
        rows = pl.ds(pl.multiple_of(kj * t, t), t)
        s_all = [jnp.dot(k_ref[rows, heads[h]], qq_scr[h], preferred_element_type=F32) for h in range(A_HEADS)]
        p_all, alpha_all = [], []
        for h in range(A_HEADS):
            s = s_all[h] + bias_scr[h]
            if masked:
                s = jnp.where(jnp.concatenate([rc, rc], axis=1) >= 0.0, s, NEG_BIG)
            off = slopes[h] * dist
            m_prev = m_scr[h]
            m_new = jnp.maximum(m_prev, jnp.max(s, axis=0, keepdims=True) - off)
            alpha = jnp.exp(m_prev - m_new)
            p = jnp.exp(s - (m_new + off))
            l_scr[h] = alpha * l_scr[h] + jnp.sum(p, axis=0, keepdims=True)
            m_scr[h] = m_new
            p_all.append(p.astype(BF16))
            alpha_all.append(alpha)
        for h in range(A_HEADS):
            acc_scr[h] = alpha_all[h] * acc_scr[h] + jnp.dot(vt_scr[h, kj], p_all[h], preferred_element_type=F32)

    def body(kj, carry):
        tile(kj, ((qi - kj) * t).astype(F32), False)
        return carry

    lax.fori_loop(0, qi, body, 0)
    tile(qi, 0.0, True)

    for h in range(A_HEADS):
        o = acc_scr[h] / l_scr[h]
        out = o[:, 0:t] - lam * o[:, t:2 * t]
        ms = jnp.mean(out * out, axis=0, keepdims=True)
        out = (out * lax.rsqrt(ms + SUBLN_EPS)) * gt_ref[...] * (1.0 - lam_init)
        o_ref[:, heads[h]] = jnp.transpose(out).astype(BF16)


def _dattn_prompt(proj, lam, subln_g, b, t_seq, lam_init):
    t = min(ATTN_T, t_seq)
    nq = t_seq // t
    r = np.arange(t, dtype=np.float32)
    rc = jnp.asarray(r[None, :] - r[:, None])
    gt = jnp.broadcast_to(subln_g.astype(F32)[:, None], (A_DV, t))
    slopes = [float(s) for s in _alibi_slopes()]
    return pl.pallas_call(
        functools.partial(_dattn_prompt_kernel, t=t, lam_init=lam_init, slopes=slopes),
        grid=(b, nq),
        in_specs=[pl.BlockSpec(memory_space=pltpu.SMEM),
                  pl.BlockSpec((t, A_QK), lambda bi, qi: (bi * nq + qi, OFF_AQ // A_QK)),
                  pl.BlockSpec((t_seq, A_QK), lambda bi, qi: (bi, OFF_AK // A_QK)),
                  pl.BlockSpec((t_seq, A_V), lambda bi, qi: (bi, OFF_AV // A_V)),
                  pl.BlockSpec((A_DV, t), lambda bi, qi: (0, 0)),
                  pl.BlockSpec((t, t), lambda bi, qi: (0, 0))],
        out_specs=pl.BlockSpec((t, A_V), lambda bi, qi: (bi * nq + qi, 0)),
        out_shape=jax.ShapeDtypeStruct((b * t_seq, A_V), BF16),
        scratch_shapes=[pltpu.VMEM((A_HEADS, A_DV, 2 * t), BF16),
                        pltpu.VMEM((A_HEADS, t, 2 * t), F32),
                        pltpu.VMEM((A_HEADS, nq, A_DV, t), BF16),
                        pltpu.VMEM((A_HEADS, 1, 2 * t), F32),
                        pltpu.VMEM((A_HEADS, 1, 2 * t), F32),
                        pltpu.VMEM((A_HEADS, A_DV, 2 * t), F32)],
        compiler_params=_cparams("parallel", "arbitrary"),
        name="dattn_prompt",
    )(lam.reshape(1), proj, proj, proj, gt, rc)


def _pattn_consts(tn, past_len):
    slopes = _alibi_slopes()
    nrow = 2 * A_HEADS * tn
    slope_row = np.array([slopes[(r // tn) % A_HEADS] for r in range(nrow)])[:, None]
    q_row = np.array([r % tn for r in range(nrow)], np.float64)[:, None]
    key = np.arange(PAGE_SIZE, dtype=np.float64)[None, :]
    page0 = slope_row * key - slope_row * (past_len + q_row)
    per_page = np.broadcast_to(slope_row * PAGE_SIZE, (nrow, PAGE_SIZE))
    new = np.where((key < tn) & (key <= q_row), -slope_row * (q_row - key), NEG_BIG)
    return jnp.asarray(page0, F32), jnp.asarray(per_page, F32), jnp.asarray(new, F32)


def _pattn_kernel(pt_ref, lam_ref, qbd_ref, kn_ref, vn_ref, g_ref, page0_ref, perpage_ref, newb_ref, ck_ref, cv_ref,
                  o_ref, kbuf, vbuf, sem, kn_scr, vn_scr, m_scr, l_scr, acc_scr, *, npg, tn, lam_init):
    j = pl.program_id(1)
    nj = pl.num_programs(1)
    step = pl.program_id(0) * nj + j
    slot = step % 2

    def page_copies(st, sl):
        cps = []
        for p_i in range(npg):
            pg = pt_ref[st * npg + p_i]
            cps.append(pltpu.make_async_copy(ck_ref.at[pg], kbuf.at[sl, p_i], sem.at[sl]))
            cps.append(pltpu.make_async_copy(cv_ref.at[pg], vbuf.at[sl, p_i], sem.at[sl]))
        return cps

    @pl.when(step == 0)
    def _():
        for cp in page_copies(0, 0):
            cp.start()

    @pl.when(step + 1 < pl.num_programs(0) * nj)
    def _():
        for cp in page_copies(step + 1, 1 - slot):
            cp.start()

    qbd = qbd_ref[...]
    nrow = qbd.shape[0]
    wide = lambda x: jnp.concatenate([x] * A_HEADS, axis=1)

    @pl.when(step == 0)
    def _():
        kn_scr[...] = jnp.zeros_like(kn_scr)
        vn_scr[...] = jnp.zeros_like(vn_scr)

    @pl.when(j == 0)
    def _():
        kn_scr[0:8, :] = kn_ref[...]
        vn_scr[0:8, :] = vn_ref[...]
        newb = newb_ref[...]
        s = lax.dot_general(qbd, kn_scr[...].astype(BF16), _NT, preferred_element_type=F32)
        s = jnp.where(newb > 0.5 * NEG_BIG, s + newb, NEG_BIG)
        m0 = jnp.broadcast_to(jnp.max(s, axis=1, keepdims=True), s.shape)
        p = jnp.exp(s - m0)
        m_scr[...] = m0
        l_scr[...] = p
        acc_scr[...] = jnp.dot(p.astype(BF16), vn_scr[...].astype(BF16), preferred_element_type=F32)

    for cp in page_copies(step, slot):
        cp.wait()

    page0 = page0_ref[...]
    per_page = perpage_ref[...]
    s_all = []
    for p_i in range(npg):
        page = (j * npg + p_i).astype(F32)
        s = jnp.dot(qbd, kbuf[slot, p_i].astype(BF16), preferred_element_type=F32)
        s_all.append(s + (page0 + per_page * page))
    mx = s_all[0]
    for s in s_all[1:]:
        mx = jnp.maximum(mx, s)
    m_prev = m_scr[...]
    m_new = jnp.maximum(m_prev, jnp.broadcast_to(jnp.max(mx, axis=1, keepdims=True), mx.shape))
    alpha = jnp.exp(m_prev - m_new)
    lsum = jnp.zeros((nrow, PAGE_SIZE), F32)
    pv = [jnp.zeros((nrow, A_DV), F32) for _ in range(A_HEADS)]
    for p_i in range(npg):
        p = jnp.exp(s_all[p_i] - m_new)
        lsum = lsum + p
        pb = p.astype(BF16)
        for h in range(A_HEADS):
            vh = vbuf[slot, p_i, pl.ds(h, PAGE_SIZE, stride=A_HEADS), :].astype(BF16)
            pv[h] = pv[h] + jnp.dot(pb, vh, preferred_element_type=F32)
    m_scr[...] = m_new
    l_scr[...] = alpha * l_scr[...] + lsum
    acc_scr[...] = wide(alpha) * acc_scr[...] + jnp.concatenate(pv, axis=1)

    @pl.when(j == nj - 1)
    def _():
        l_tot = jnp.sum(l_scr[...], axis=1, keepdims=True)
        lam = lam_ref[0]
        g = g_ref[...]
        half = A_HEADS * tn
        for h in range(A_HEADS):
            cols = slice(h * A_DV, (h + 1) * A_DV)
            blk = acc_scr[:, cols] / l_tot
            out = blk[h * tn:(h + 1) * tn] - lam * blk[half + h * tn:half + (h + 1) * tn]
            o_ref[:, cols] = _subln(out, g, lam_init)


def _dattn_sample(qbd, k_new, v_new, cache_k, cache_v, page_table, lam, subln_g, tn, lam_init):
    db, n_pages = page_table.shape
    npg = min(PAGES_PER_STEP, n_pages)
    nrow = 2 * A_HEADS * tn
    assert nrow % 16 == 0 and tn <= 8
    page0, per_page, newb = _pattn_consts(tn, n_pages * PAGE_SIZE)
    const = lambda shape: pl.BlockSpec(shape, lambda bi, j, pt: (0,) * len(shape))
    grid_spec = pltpu.PrefetchScalarGridSpec(
        num_scalar_prefetch=1,
        grid=(db, n_pages // npg),
        in_specs=[pl.BlockSpec(memory_space=pltpu.SMEM),
                  pl.BlockSpec((None, nrow, A_QK), lambda bi, j, pt: (bi, 0, 0)),
                  pl.BlockSpec((None, 8, A_QK), lambda bi, j, pt: (bi, 0, 0)),
                  pl.BlockSpec((None, 8, A_V), lambda bi, j, pt: (bi, 0, 0)),
                  const((1, A_DV)), const((nrow, PAGE_SIZE)), const((nrow, PAGE_SIZE)), const((nrow, PAGE_SIZE)),
                  pl.BlockSpec(memory_space=pl.ANY), pl.BlockSpec(memory_space=pl.ANY)],
        out_specs=pl.BlockSpec((None, tn, A_V), lambda bi, j, pt: (bi, 0, 0)),
        scratch_shapes=[pltpu.VMEM((2, npg, A_QK, PAGE_SIZE), F32),
                        pltpu.VMEM((2, npg, PAGE_SIZE * A_HEADS, A_DV), F32),
                        pltpu.SemaphoreType.DMA((2,)),
                        pltpu.VMEM((PAGE_SIZE, A_QK), F32), pltpu.VMEM((PAGE_SIZE, A_V), F32),
                        pltpu.VMEM((nrow, PAGE_SIZE), F32), pltpu.VMEM((nrow, PAGE_SIZE), F32),
                        pltpu.VMEM((nrow, A_V), F32)],
    )
    return pl.pallas_call(
        functools.partial(_pattn_kernel, npg=npg, tn=tn, lam_init=lam_init),
        grid_spec=grid_spec,
        out_shape=jax.ShapeDtypeStruct((db, tn, A_V), F32),
        compiler_params=_cparams("arbitrary", "arbitrary"),
        name="dattn_sample",
    )(page_table.reshape(-1), lam.reshape(1), qbd, k_new, v_new, subln_g.reshape(1, A_DV), page0, per_page, newb,
      cache_k, cache_v)


ROUTER_ROWS = 8 + N_EXPERTS


def _merge_kernel(x_ref, ret_ref, dif_ref, ga0_ref, ga1_ref, gb0_ref, gb1_ref, wro_ref, wdo_ref, wo_ref, gf_ref,
                  wr_ref, br_ref, h_ref, xn_ref, re_ref, rw_ref):
    a = jnp.dot(ret_ref[...], wro_ref[...], preferred_element_type=F32)
    d = jnp.dot(dif_ref[...], wdo_ref[...], preferred_element_type=F32)
    ga = jnp.concatenate([ga0_ref[...], ga1_ref[...]], axis=1).astype(F32)
    gb = jnp.concatenate([gb0_ref[...], gb1_ref[...]], axis=1).astype(F32)
    m = jax.nn.sigmoid(ga) * a + jax.nn.sigmoid(gb) * d
    h1 = x_ref[...] + jnp.dot(m.astype(BF16), wo_ref[...], preferred_element_type=F32)
    h_ref[...] = h1
    ms = jnp.mean(h1 * h1, axis=-1, keepdims=True)
    xn = h1 * lax.rsqrt(ms + EPS) * gf_ref[...]
    xn_ref[...] = xn

    lt = lax.dot_general(wr_ref[...], xn.astype(BF16), _NT, preferred_element_type=F32) + br_ref[:, 0:1]
    bm = lt.shape[1]
    row8 = lax.broadcasted_iota(jnp.int32, (8, bm), 0)

    def first_argmax(v):
        top = jnp.max(v, axis=0, keepdims=True)
        return top, jnp.min(jnp.where(v == top, row8, 8), axis=0, keepdims=True)

    lg = jnp.where(row8 < N_GROUPS, lt[0:8], NEG_BIG)
    gmax, g_sel = first_argmax(lg)
    g_w = 1.0 / jnp.sum(jnp.exp(lg - gmax), axis=0, keepdims=True)
    el = jnp.zeros((8, bm), F32)
    for g in range(N_GROUPS):
        el = jnp.where(g_sel == g, lt[8 + 8 * g:16 + 8 * g], el)
    ee = jnp.exp(el - jnp.max(el, axis=0, keepdims=True))
    ep = ee / jnp.sum(ee, axis=0, keepdims=True)
    p1, i1 = first_argmax(ep)
    p2, i2 = first_argmax(jnp.where(row8 == i1, -1.0, ep))
    den = p1 + p2
    re_ref[0:1, :] = g_sel * EXP_PER_GROUP + i1
    re_ref[1:2, :] = g_sel * EXP_PER_GROUP + i2
    rw_ref[0:1, :] = g_w * p1 / den
    rw_ref[1:2, :] = g_w * p2 / den


def _merge(x, ret, dif, proj, w_ret_out, w_diff_out, w_out, norm_ffn, w_router, b_router):
    m = x.shape[0]
    bm = min(m, 512)
    const = lambda shape: pl.BlockSpec(shape, lambda i: (0,) * len(shape))
    half = D_MODEL // 2
    gate = lambda off: pl.BlockSpec((bm, half), lambda i: (i, off // half))
    return pl.pallas_call(
        _merge_kernel,
        grid=(m // bm,),
        in_specs=[pl.BlockSpec((bm, D_MODEL), lambda i: (i, 0)),
                  pl.BlockSpec((bm, R_V), lambda i: (i, 0)),
                  pl.BlockSpec((bm, A_V), lambda i: (i, 0)),
                  gate(OFF_GA), gate(OFF_GA + half), gate(OFF_GB), gate(OFF_GB + half),
                  const((R_V, D_MODEL)), const((A_V, D_MODEL)), const((D_MODEL, D_MODEL)),
                  const((1, D_MODEL)), const((ROUTER_ROWS, D_MODEL)), const((ROUTER_ROWS, 128))],
        out_specs=[pl.BlockSpec((bm, D_MODEL), lambda i: (i, 0)),
                   pl.BlockSpec((bm, D_MODEL), lambda i: (i, 0)),
                   pl.BlockSpec((2, bm), lambda i: (0, i)),
                   pl.BlockSpec((2, bm), lambda i: (0, i))],
        out_shape=[jax.ShapeDtypeStruct((m, D_MODEL), F32),
                   jax.ShapeDtypeStruct((m, D_MODEL), F32),
                   jax.ShapeDtypeStruct((2, m), jnp.int32),
                   jax.ShapeDtypeStruct((2, m), F32)],
        compiler_params=_cparams("parallel"),
        name="merge",
    )(x, ret, dif, proj, proj, proj, proj, w_ret_out, w_diff_out, w_out, norm_ffn.reshape(1, D_MODEL),
      w_router, b_router)


def _plan_kernel(re_ref, tri_ref, rank_ref, cnt_ref, carry_scr):
    @pl.when(pl.program_id(0) == 0)
    def _():
        carry_scr[...] = jnp.zeros_like(carry_scr)

    tb = re_ref.shape[1]
    erow = lax.broadcasted_iota(jnp.int32, (N_EXPERTS, tb), 0)
    oh0 = jnp.where(erow == re_ref[0:1, :], 1.0, 0.0)
    oh1 = jnp.where(erow == re_ref[1:2, :], 1.0, 0.0)
    both = oh0 + oh1
    before = jnp.dot(both.astype(BF16), tri_ref[...], preferred_element_type=F32) + carry_scr[:, 0:1]
    rank_ref[0:1, :] = jnp.sum(oh0 * before, axis=0, keepdims=True).astype(jnp.int32)
    rank_ref[1:2, :] = jnp.sum(oh1 * before, axis=0, keepdims=True).astype(jnp.int32)
    carry_scr[...] = carry_scr[...] + jnp.sum(both, axis=1, keepdims=True)
    cnt_ref[...] = carry_scr[...]


def _plan(expert_id):
    m = expert_id.shape[1]
    tb = min(m, 512)
    tri = jnp.asarray(np.triu(np.ones((tb, tb), np.float32), 1), BF16)
    return pl.pallas_call(
        _plan_kernel,
        grid=(m // tb,),
        in_specs=[pl.BlockSpec((2, tb), lambda i: (0, i)),
                  pl.BlockSpec((tb, tb), lambda i: (0, 0))],
        out_specs=[pl.BlockSpec((2, tb), lambda i: (0, i)),
                   pl.BlockSpec((N_EXPERTS, 128), lambda i: (0, 0))],
        out_shape=[jax.ShapeDtypeStruct((2, m), jnp.int32),
                   jax.ShapeDtypeStruct((N_EXPERTS, 128), F32)],
        scratch_shapes=[pltpu.VMEM((N_EXPERTS, 128), F32)],
        compiler_params=_cparams("arbitrary"),
        name="moe_plan",
    )(expert_id, tri)


def _dispatch_kernel(dest_ref, pad_ref, nu_ref, x_ref, xs_ref, zero_scr, sem, zsem):
    tb = x_ref.shape[0]

    def row_copy(t, s):
        return pltpu.make_async_copy(x_ref.at[pl.ds(t, 1)], xs_ref.at[pl.ds(dest_ref[s, t], 1)], sem)

    def issue(t, carry):
        row_copy(t, 0).start()
        row_copy(t, 1).start()
        return carry

    lax.fori_loop(0, tb, issue, 0, unroll=8)

    @pl.when(pl.program_id(0) == 0)
    def _():
        zero_scr[...] = jnp.zeros_like(zero_scr)

        def pad_expert(e, carry):
            start, n = pad_ref[0, e], pad_ref[1, e]
            zero_row = lambda r: pltpu.make_async_copy(
                zero_scr.at[pl.ds(0, 1)], xs_ref.at[pl.ds(start + r, 1)], zsem)
            lax.fori_loop(0, n, lambda r, c: (zero_row(r).start(), c)[1], 0)
            lax.fori_loop(0, n, lambda r, c: (zero_row(r).wait(), c)[1], 0)
            return carry

        lax.fori_loop(0, N_EXPERTS, pad_expert, 0)
        nblk = xs_ref.shape[0] // MOE_TM
        zero_blk = lambda i: pltpu.make_async_copy(
            zero_scr, xs_ref.at[pl.ds(pl.multiple_of(i * MOE_TM, MOE_TM), MOE_TM)], zsem)
        lax.fori_loop(nu_ref[0], nblk, lambda i, c: (zero_blk(i).start(), c)[1], 0)
        lax.fori_loop(nu_ref[0], nblk, lambda i, c: (zero_blk(i).wait(), c)[1], 0)

    def drain(t, carry):
        row_copy(t, 0).wait()
        row_copy(t, 1).wait()
        return carry

    lax.fori_loop(0, tb, drain, 0, unroll=8)


def _dispatch(xn, dest, pad_info, n_used, rows):
    m = xn.shape[0]
    tb = min(m, 512)
    return pl.pallas_call(
        _dispatch_kernel,
        grid=(m // tb,),
        in_specs=[pl.BlockSpec((2, tb), lambda i: (0, i), memory_space=pltpu.SMEM),
                  pl.BlockSpec(memory_space=pltpu.SMEM),
                  pl.BlockSpec(memory_space=pltpu.SMEM),
                  pl.BlockSpec((tb, D_MODEL), lambda i: (i, 0))],
        out_specs=pl.BlockSpec(memory_space=pl.ANY),
        out_shape=jax.ShapeDtypeStruct((rows, D_MODEL), F32),
        scratch_shapes=[pltpu.VMEM((MOE_TM, D_MODEL), F32),
                        pltpu.SemaphoreType.DMA(()), pltpu.SemaphoreType.DMA(())],
        compiler_params=_cparams("arbitrary"),
        name="moe_dispatch",
    )(dest, pad_info, n_used, xn)


def _expert_kernel(be_ref, nu_ref, x_ref, wg_ref, wu_ref, wd_ref, y_ref):
    live = pl.program_id(0) < nu_ref[0]

    @pl.when(jnp.logical_not(live))
    def _():
        y_ref[...] = jnp.zeros_like(y_ref)

    @pl.when(live)
    def _():
        x = x_ref[...].astype(BF16)
        gate = jnp.dot(x, wg_ref[...], preferred_element_type=F32)
        up = jnp.dot(x, wu_ref[...], preferred_element_type=F32)
        hid = (gate * jax.nn.sigmoid(gate)) * up
        y_ref[...] = jnp.dot(hid.astype(BF16), wd_ref[...], preferred_element_type=F32)


def _experts(xs, block_e, n_used, w_gate, w_up, w_down):
    rows = xs.shape[0]
    nblk = rows // MOE_TM
    live = lambda i, be, nu: jnp.minimum(i, nu[0] - 1)
    grid_spec = pltpu.PrefetchScalarGridSpec(
        num_scalar_prefetch=2,
        grid=(nblk,),
        in_specs=[pl.BlockSpec((MOE_TM, D_MODEL), lambda i, be, nu: (live(i, be, nu), 0)),
                  pl.BlockSpec((None, D_MODEL, D_EXPERT), lambda i, be, nu: (be[live(i, be, nu)], 0, 0)),
                  pl.BlockSpec((None, D_MODEL, D_EXPERT), lambda i, be, nu: (be[live(i, be, nu)], 0, 0)),
                  pl.BlockSpec((None, D_EXPERT, D_MODEL), lambda i, be, nu: (be[live(i, be, nu)], 0, 0))],
        out_specs=pl.BlockSpec((MOE_TM, D_MODEL), lambda i, be, nu: (i, 0)),
    )
    return pl.pallas_call(
        _expert_kernel,
        grid_spec=grid_spec,
        out_shape=jax.ShapeDtypeStruct((rows, D_MODEL), F32),
        compiler_params=_cparams("arbitrary"),
        name="moe_experts",
    )(block_e, n_used, xs, w_gate, w_up, w_down)


def _combine_kernel(dest_ref, h_ref, w_ref, gf_ref, yb_ref, y_ref, g_scr, sem):
    tb = h_ref.shape[0]

    def row_copy(t, s):
        return pltpu.make_async_copy(yb_ref.at[pl.ds(dest_ref[s, t], 1)], g_scr.at[s, pl.ds(t, 1)], sem)

    def issue(t, carry):
        row_copy(t, 0).start()
        row_copy(t, 1).start()
        return carry

    def drain(t, carry):
        row_copy(t, 0).wait()
        row_copy(t, 1).wait()
        return carry

    lax.fori_loop(0, tb, issue, 0, unroll=8)
    lax.fori_loop(0, tb, drain, 0, unroll=8)
    y = h_ref[...] + (w_ref[:, 0:1] * g_scr[0] + w_ref[:, 1:2] * g_scr[1])
    ms = jnp.mean(y * y, axis=-1, keepdims=True)
    y_ref[...] = y * lax.rsqrt(ms + EPS) * gf_ref[...]


def _combine(h1, yb, dest, weights_t, norm_final):
    m = h1.shape[0]
    tb = min(m, 256)
    return pl.pallas_call(
        _combine_kernel,
        grid=(m // tb,),
        in_specs=[pl.BlockSpec((2, tb), lambda i: (0, i), memory_space=pltpu.SMEM),
                  pl.BlockSpec((tb, D_MODEL), lambda i: (i, 0)),
                  pl.BlockSpec((tb, 2), lambda i: (i, 0)),
                  pl.BlockSpec((1, D_MODEL), lambda i: (0, 0)),
                  pl.BlockSpec(memory_space=pl.ANY)],
        out_specs=pl.BlockSpec((tb, D_MODEL), lambda i: (i, 0)),
        out_shape=jax.ShapeDtypeStruct((m, D_MODEL), F32),
        scratch_shapes=[pltpu.VMEM((2, tb, D_MODEL), F32), pltpu.SemaphoreType.DMA(())],
        compiler_params=_cparams("arbitrary"),
        name="moe_combine",
    )(dest, h1, weights_t, norm_final.reshape(1, D_MODEL), yb)


def _moe_and_final(h1, xn, expert_id, weights, w_gate, w_up, w_down, norm_final):
    m = h1.shape[0]
    rank, counts = _plan(expert_id)
    counts = counts[:, 0].astype(jnp.int32)
    padded = (counts + MOE_TM - 1) // MOE_TM * MOE_TM
    pad_end = jnp.cumsum(padded)
    pad_start = pad_end - padded
    is_e = expert_id[:, :, None] == jnp.arange(N_EXPERTS, dtype=jnp.int32)
    dest = jnp.sum(jnp.where(is_e, pad_start, 0), axis=-1) + rank
    nblk = (m * TOP_K) // MOE_TM + N_EXPERTS
    block_start = jnp.arange(nblk, dtype=jnp.int32) * MOE_TM
    block_e = jnp.minimum(jnp.sum((block_start[:, None] >= pad_end[None, :]).astype(jnp.int32), axis=1),
                          N_EXPERTS - 1).astype(jnp.int32)
    n_used = (pad_end[-1:] // MOE_TM).astype(jnp.int32)
    pad_info = jnp.stack([pad_start + counts, padded - counts]).astype(jnp.int32)
    xs = _dispatch(xn, dest, pad_info, n_used, nblk * MOE_TM)
    yb = _experts(xs, block_e, n_used, w_gate, w_up, w_down)
    return _combine(h1, yb, dest, weights.T, norm_final)


def kernel(x_prompt, x_sample, state_ret, cache_k, cache_v, page_table, norm_mix, w_in, lambda_q1, lambda_k1,
           lambda_q2, lambda_k2, subln_g, w_ret_out, w_diff_out, w_out, norm_ffn, w_group, b_group, w_expert,
           b_expert, w_up, w_gate, w_down, norm_final):
    b, t_seq, _ = x_prompt.shape
    db, tn, _ = x_sample.shape
    depth = w_in.shape[0]
    assert depth == 1 and tn <= 8
    n_phys = cache_k.shape[1]
    l = 0
    lam_init = 0.8 - 0.6 * float(np.exp(-0.3 * l))
    lam = (jnp.exp(jnp.sum(lambda_q1[l].astype(F32) * lambda_k1[l].astype(F32)))
           - jnp.exp(jnp.sum(lambda_q2[l].astype(F32) * lambda_k2[l].astype(F32))) + lam_init)

    w_in_b = w_in[l].astype(BF16)
    wkt_b = w_in[l][:, OFF_AK:OFF_AK + A_QK].T.astype(BF16)
    wro, wdo, wo = w_ret_out[l].astype(BF16), w_diff_out[l].astype(BF16), w_out[l].astype(BF16)
    wg_b, wu_b, wd_b = w_gate[l].astype(BF16), w_up[l].astype(BF16), w_down[l].astype(BF16)
    w_router = jnp.zeros((ROUTER_ROWS, D_MODEL), F32)
    w_router = w_router.at[0:N_GROUPS].set(w_group[l].T).at[8:].set(w_expert[l].T).astype(BF16)
    b_router = jnp.zeros((ROUTER_ROWS,), F32).at[0:N_GROUPS].set(b_group[l]).at[8:].set(b_expert[l])
    b_router = jnp.broadcast_to(b_router[:, None], (ROUTER_ROWS, 128))

    def tail(x, ret, dif, proj):
        h1, xn, eid, wts = _merge(x, ret, dif, proj, wro, wdo, wo, norm_ffn[l], w_router, b_router)
        return _moe_and_final(h1, xn, eid, wts, wg_b, wu_b, wd_b, norm_final)

    xp = x_prompt.reshape(b * t_seq, D_MODEL)
    proj_p, kt_p, vr_p = _proj(xp, norm_mix[l], w_in_b, wkt_b, seq_len=t_seq)
    ret_p, s_p = _ret_prompt(proj_p, b, t_seq)
    dif_p = _dattn_prompt(proj_p, lam, subln_g[l], b, t_seq, lam_init)
    y_p = tail(xp, ret_p, dif_p, proj_p)

    xs = x_sample.reshape(db * tn, D_MODEL)
    proj_s, k_s, v_s = _proj(xs, norm_mix[l], w_in_b, wkt_b)
    ret_s, s_s = _ret_sample(proj_s, state_ret[l], tn)
    aq = proj_s[:, OFF_AQ:OFF_AQ + A_QK].astype(F32).reshape(db, tn, A_HEADS, 2, A_DH) * (A_DH ** -0.5)
    eye_h = jnp.eye(A_HEADS, dtype=F32)
    eye_c = jnp.eye(2, dtype=F32)
    qbd = jnp.einsum('bqhcd,hH,cC->bCHqhcd', aq, eye_h, eye_c).reshape(db, 2 * A_HEADS * tn, A_QK).astype(BF16)
    pad_new = lambda a: jnp.pad(a.reshape(db, tn, -1), ((0, 0), (0, 8 - tn), (0, 0)))
    ck_pages = jnp.transpose(cache_k[l], (0, 2, 3, 4, 1)).reshape(n_phys, A_QK, PAGE_SIZE)
    cv_pages = cache_v[l].reshape(n_phys, PAGE_SIZE * A_HEADS, A_DV)
    dif_s = _dattn_sample(qbd, pad_new(k_s), pad_new(v_s), ck_pages, cv_pages,
                          page_table.astype(jnp.int32), lam, subln_g[l], tn, lam_init)
    y_s = tail(xs, ret_s, dif_s.reshape(db * tn, A_V).astype(BF16), proj_s)

    return (y_p.reshape(b, t_seq, D_MODEL),
            y_s.reshape(db, tn, D_MODEL),
            s_p[None],
            s_s[None],
            jnp.transpose(kt_p.reshape(1, b, A_HEADS, 2, A_DH, t_seq), (0, 1, 5, 2, 3, 4)),
            vr_p.reshape(1, b, t_seq, A_HEADS, A_DV),
            k_s.reshape(1, db, tn, A_HEADS, 2, A_DH),
            v_s.reshape(1, db, tn, A_HEADS, A_DV))
```

```python
import functools

import numpy as np
import jax
import jax.numpy as jnp
from jax import lax
from jax.experimental import pallas as pl
from jax.experimental.pallas import tpu as pltpu

D_MODEL = 1024
R_HEADS, R_DK, R_DV = 4, 128, 256
R_CHUNK = 256
A_HEADS, A_DH, A_DV = 4, 64, 128
N_GROUPS, EXP_PER_GROUP, TOP_K, D_EXPERT = 4, 8, 2, 512
N_EXPERTS = N_GROUPS * EXP_PER_GROUP
PAGE_SIZE = 128
EPS, SUBLN_EPS, NEG_BIG = 1e-6, 1e-5, -1e30

R_QK, R_V = R_HEADS * R_DK, R_HEADS * R_DV
A_QK, A_V = A_HEADS * 2 * A_DH, A_HEADS * A_DV
OFF_RQ, OFF_RK, OFF_RV, OFF_RG = 0, R_QK, 2 * R_QK, 2 * R_QK + R_V
OFF_AQ = OFF_RG + R_V
OFF_AK, OFF_AV = OFF_AQ + A_QK, OFF_AQ + 2 * A_QK
OFF_GA = OFF_AV + A_V
OFF_GB = OFF_GA + D_MODEL
N_IN = OFF_GB + D_MODEL

PROJ_BM = 512
PROJ_BN = 512
ATTN_T = 256
PAGES_PER_STEP = 16
MOE_TM = 256
VMEM_LIMIT = 56 * 1024 * 1024

F32, BF16 = jnp.float32, jnp.bfloat16
_NT = (((1,), (1,)), ((), ()))
_TN = (((0,), (0,)), ((), ()))


def _cparams(*sem):
    return pltpu.CompilerParams(dimension_semantics=sem, vmem_limit_bytes=VMEM_LIMIT)


def _log_decay():
    return np.log1p(-np.exp2(-5.0 - np.arange(R_HEADS, dtype=np.float64)))


def _alibi_slopes():
    return np.exp2(-8.0 * np.arange(1, A_HEADS + 1, dtype=np.float64) / A_HEADS)


def _proj_kernel(x_ref, g_ref, w_ref, wkt_ref, p_ref, k_ref, v_ref, xn_ref, *, seq_major):
    x = x_ref[...]
    ms = jnp.mean(x * x, axis=-1, keepdims=True)
    xn_ref[...] = (x * lax.rsqrt(ms + EPS) * g_ref[...]).astype(BF16)
    for j in range(N_IN // PROJ_BN):
        cols = slice(j * PROJ_BN, (j + 1) * PROJ_BN)
        acc = jnp.dot(xn_ref[...], w_ref[:, cols], preferred_element_type=F32)
        p_ref[:, cols] = acc.astype(BF16)
        if j == OFF_AK // PROJ_BN:
            if seq_major:
                k_ref[...] = lax.dot_general(wkt_ref[...], xn_ref[...], _NT, preferred_element_type=F32)
            else:
                k_ref[...] = acc
        if j == OFF_AV // PROJ_BN:
            if seq_major:
                for h in range(A_HEADS):
                    v_ref[pl.ds(h, acc.shape[0], stride=A_HEADS), :] = acc[:, h * A_DV:(h + 1) * A_DV]
            else:
                v_ref[...] = acc


def _proj(x, gain, w_bf16, wkt_bf16, seq_len=None):
    m = x.shape[0]
    bm = min(m, PROJ_BM) if seq_len is None else min(seq_len, PROJ_BM)
    if seq_len is None:
        kv_specs = [pl.BlockSpec((bm, A_QK), lambda i: (i, 0)),
                    pl.BlockSpec((bm, A_V), lambda i: (i, 0))]
        kv_shapes = [jax.ShapeDtypeStruct((m, A_QK), F32), jax.ShapeDtypeStruct((m, A_V), F32)]
    else:
        per_seq = seq_len // bm
        kv_specs = [pl.BlockSpec((None, A_QK, bm), lambda i: (i // per_seq, 0, i % per_seq)),
                    pl.BlockSpec((bm * A_HEADS, A_DV), lambda i: (i, 0))]
        kv_shapes = [jax.ShapeDtypeStruct((m // seq_len, A_QK, seq_len), F32),
                     jax.ShapeDtypeStruct((m * A_HEADS, A_DV), F32)]
    resident = lambda shape: pl.BlockSpec(shape, lambda i: (0, 0), pipeline_mode=pl.Buffered(1))
    return pl.pallas_call(
        functools.partial(_proj_kernel, seq_major=seq_len is not None),
        grid=(m // bm,),
        in_specs=[pl.BlockSpec((bm, D_MODEL), lambda i: (i, 0)),
                  resident((1, D_MODEL)), resident((D_MODEL, N_IN)), resident((A_QK, D_MODEL))],
        out_specs=[pl.BlockSpec((bm, N_IN), lambda i: (i, 0))] + kv_specs,
        out_shape=[jax.ShapeDtypeStruct((m, N_IN), BF16)] + kv_shapes,
        scratch_shapes=[pltpu.VMEM((bm, D_MODEL), BF16)],
        compiler_params=_cparams("parallel"),
        name="proj",
    )(x, gain.reshape(1, D_MODEL), w_bf16, wkt_bf16)


def _ret_consts(c, rows_per_seq):
    lg = _log_decay()
    pos = np.arange(c, dtype=np.float64)
    dist = pos[:, None] - pos[None, :]
    n = c * rows_per_seq
    intra = np.zeros((R_HEADS, n, n), np.float32)
    qdec = np.zeros((n, R_QK), np.float32)
    kdec = np.zeros((n, R_QK), np.float32)
    for h in range(R_HEADS):
        blk = np.where(dist >= 0, np.exp(np.maximum(dist, 0.0) * lg[h]), 0.0)
        for s in range(rows_per_seq):
            intra[h, s * c:(s + 1) * c, s * c:(s + 1) * c] = blk
            qdec[s * c:(s + 1) * c, h * R_DK:(h + 1) * R_DK] = np.exp((pos + 1.0) * lg[h])[:, None]
            kdec[s * c:(s + 1) * c, h * R_DK:(h + 1) * R_DK] = (
                np.exp((c - 1.0 - pos) * lg[h])[:, None] * R_DK ** -0.5)
    cdec = [float(np.float32(np.exp(c * lg[h]))) for h in range(R_HEADS)]
    return jnp.asarray(intra), jnp.asarray(qdec), jnp.asarray(kdec), cdec


def _gated_norm(o, g):
    ms = jnp.mean(o * o, axis=-1, keepdims=True)
    return (o * lax.rsqrt(ms + EPS)) * (g * jax.nn.sigmoid(g))


def _ret_prompt_kernel(q_ref, k_ref, v_ref, rg_ref, intra_ref, qdec_ref, kdec_ref,
                       ret_ref, sout_ref, s_scr, *, cdec):
    c = pl.program_id(1)

    @pl.when(c == 0)
    def _():
        s_scr[...] = jnp.zeros_like(s_scr)

    q = q_ref[...]
    kf = k_ref[...].astype(F32)
    qd = (q.astype(F32) * qdec_ref[...]).astype(BF16)
    kd = (kf * kdec_ref[...]).astype(BF16)
    ks = (kf * (R_DK ** -0.5)).astype(BF16)
    for h in range(R_HEADS):
        hk = slice(h * R_DK, (h + 1) * R_DK)
        hv = slice(h * R_DV, (h + 1) * R_DV)
        vh = v_ref[:, hv]
        s_old = s_scr[h]
        sc = lax.dot_general(q[:, hk], ks[:, hk], _NT, preferred_element_type=F32) * intra_ref[h]
        o = (jnp.dot(sc.astype(BF16), vh, preferred_element_type=F32)
             + jnp.dot(qd[:, hk], s_old.astype(BF16), preferred_element_type=F32))
        s_scr[h] = cdec[h] * s_old + lax.dot_general(kd[:, hk], vh, _TN, preferred_element_type=F32)
        ret_ref[:, hv] = _gated_norm(o, rg_ref[:, hv].astype(F32)).astype(BF16)

    @pl.when(c == pl.num_programs(1) - 1)
    def _():
        sout_ref[0] = s_scr[...]


def _ret_prompt(proj, b, t):
    c = min(R_CHUNK, t)
    nc = t // c
    intra, qdec, kdec, cdec = _ret_consts(c, 1)
    row = lambda bi, ci: bi * nc + ci
    return pl.pallas_call(
        functools.partial(_ret_prompt_kernel, cdec=cdec),
        grid=(b, nc),
        in_specs=[pl.BlockSpec((c, R_QK), lambda bi, ci: (row(bi, ci), OFF_RQ // R_QK)),
                  pl.BlockSpec((c, R_QK), lambda bi, ci: (row(bi, ci), OFF_RK // R_QK)),
                  pl.BlockSpec((c, R_V), lambda bi, ci: (row(bi, ci), OFF_RV // R_V)),
                  pl.BlockSpec((c, R_V), lambda bi, ci: (row(bi, ci), OFF_RG // R_V)),
                  pl.BlockSpec((R_HEADS, c, c), lambda bi, ci: (0, 0, 0)),
                  pl.BlockSpec((c, R_QK), lambda bi, ci: (0, 0)),
                  pl.BlockSpec((c, R_QK), lambda bi, ci: (0, 0))],
        out_specs=[pl.BlockSpec((c, R_V), lambda bi, ci: (row(bi, ci), 0)),
                   pl.BlockSpec((1, R_HEADS, R_DK, R_DV), lambda bi, ci: (bi, 0, 0, 0))],
        out_shape=[jax.ShapeDtypeStruct((b * t, R_V), BF16),
                   jax.ShapeDtypeStruct((b, R_HEADS, R_DK, R_DV), F32)],
        scratch_shapes=[pltpu.VMEM((R_HEADS, R_DK, R_DV), F32)],
        compiler_params=_cparams("parallel", "arbitrary"),
        name="ret_prompt",
    )(proj, proj, proj, proj, intra, qdec, kdec)


def _ret_sample_kernel(q_ref, k_ref, v_ref, rg_ref, s_ref, intra_ref, qdec_ref, kdec_ref,
                       ret_ref, sout_ref, *, cdec, tn, nb):
    q = q_ref[...]
    kf = k_ref[...].astype(F32)
    qd = (q.astype(F32) * qdec_ref[...]).astype(BF16)
    kdf = kf * kdec_ref[...]
    ks = (kf * (R_DK ** -0.5)).astype(BF16)
    rows = lax.broadcasted_iota(jnp.int32, (nb * tn, 1), 0)
    for h in range(R_HEADS):
        hk = slice(h * R_DK, (h + 1) * R_DK)
        hv = slice(h * R_DV, (h + 1) * R_DV)
        vh = v_ref[:, hv]
        sc = lax.dot_general(q[:, hk], ks[:, hk], _NT, preferred_element_type=F32) * intra_ref[h]
        o = jnp.dot(sc.astype(BF16), vh, preferred_element_type=F32)
        for s in range(nb):
            mine = (rows >= s * tn) & (rows < (s + 1) * tn)
            s_old = s_ref[s, h]
            o = o + jnp.where(mine, jnp.dot(qd[:, hk], s_old.astype(BF16), preferred_element_type=F32), 0.0)
            kd_s = jnp.where(mine, kdf[:, hk], 0.0).astype(BF16)
            sout_ref[s, h] = cdec[h] * s_old + lax.dot_general(kd_s, vh, _TN, preferred_element_type=F32)
        ret_ref[:, hv] = _gated_norm(o, rg_ref[:, hv].astype(F32)).astype(BF16)


def _ret_sample(proj, state, tn):
    db = state.shape[0]
    nb = min(db, 8)
    n = nb * tn
    intra, qdec, kdec, cdec = _ret_consts(tn, nb)
    return pl.pallas_call(
        functools.partial(_ret_sample_kernel, cdec=cdec, tn=tn, nb=nb),
        grid=(db // nb,),
        in_specs=[pl.BlockSpec((n, R_QK), lambda i: (i, OFF_RQ // R_QK)),
                  pl.BlockSpec((n, R_QK), lambda i: (i, OFF_RK // R_QK)),
                  pl.BlockSpec((n, R_V), lambda i: (i, OFF_RV // R_V)),
                  pl.BlockSpec((n, R_V), lambda i: (i, OFF_RG // R_V)),
                  pl.BlockSpec((nb, R_HEADS, R_DK, R_DV), lambda i: (i, 0, 0, 0)),
                  pl.BlockSpec((R_HEADS, n, n), lambda i: (0, 0, 0)),
                  pl.BlockSpec((n, R_QK), lambda i: (0, 0)),
                  pl.BlockSpec((n, R_QK), lambda i: (0, 0))],
        out_specs=[pl.BlockSpec((n, R_V), lambda i: (i, 0)),
                   pl.BlockSpec((nb, R_HEADS, R_DK, R_DV), lambda i: (i, 0, 0, 0))],
        out_shape=[jax.ShapeDtypeStruct((db * tn, R_V), BF16),
                   jax.ShapeDtypeStruct(state.shape, F32)],
        compiler_params=_cparams("parallel"),
        name="ret_sample",
    )(proj, proj, proj, proj, state, intra, qdec, kdec)


def _subln(o, g, lam_init):
    ms = jnp.mean(o * o, axis=-1, keepdims=True)
    return (o * lax.rsqrt(ms + SUBLN_EPS)) * g * (1.0 - lam_init)


def _dattn_prompt_kernel(lam_ref, q_ref, k_ref, v_ref, gt_ref, rc_ref, posk_ref, posq_ref, o_ref,
                         qq_scr, vt_scr, s_scr, m_scr, l_scr, acc_scr, *, t, lam_init, slopes):
    qi = pl.program_id(1)
    lam = lam_ref[0]
    heads = [slice(h * A_DV, (h + 1) * A_DV) for h in range(A_HEADS)]

    @pl.when(qi == 0)
    def _():
        for h in range(A_HEADS):
            for kj in range(v_ref.shape[0] // t):
                vt_scr[h, kj] = jnp.transpose(v_ref[kj * t:(kj + 1) * t, heads[h]].astype(F32)).astype(BF16)

    rc = rc_ref[...]
    for h in range(A_HEADS):
        qt = jnp.transpose(q_ref[:, heads[h]].astype(F32) * (A_DH ** -0.5))
        feat = lax.broadcasted_iota(jnp.int32, qt.shape, 0)
        qq_scr[h, :, 0:t] = jnp.where(feat < A_DH, qt, 0.0).astype(BF16)
        qq_scr[h, :, t:2 * t] = jnp.where(feat >= A_DH, qt, 0.0).astype(BF16)
    m_scr[...] = jnp.full_like(m_scr, NEG_BIG)
    l_scr[...] = jnp.zeros_like(l_scr)
    acc_scr[...] = jnp.zeros_like(acc_scr)

    def tile(kj, dist, masked):
        rows = pl.ds(pl.multiple_of(kj * t, t), t)
        posk = posk_ref[...]
        for h in range(A_HEADS):
            s_scr[h] = jnp.dot(jnp.concatenate([k_ref[rows, heads[h]], posk], axis=1),
                               jnp.concatenate([qq_scr[h], posq_ref[h]], axis=0), preferred_element_type=F32)
        p_all, alpha_all = [], []
        for h in range(A_HEADS):
            s = s_scr[h]
            if masked:
                s = jnp.where(jnp.concatenate([rc, rc], axis=1) >= 0.0, s, NEG_BIG)
            off = slopes[h] * dist
            m_prev = m_scr[h]
            m_new = jnp.maximum(m_prev, jnp.max(s, axis=0, keepdims=True) - off)
            alpha = jnp.exp(m_prev - m_new)
            p = jnp.exp(s - (m_new + off))
            l_scr[h] = alpha * l_scr[h] + jnp.sum(p, axis=0, keepdims=True)
            m_scr[h] = m_new
            p_all.append(p.astype(BF16))
            alpha_all.append(alpha)
        for h in range(A_HEADS):
            acc_scr[h] = alpha_all[h] * acc_scr[h] + jnp.dot(vt_scr[h, kj], p_all[h], preferred_element_type=F32)

    def body(kj, carry):
        tile(kj, ((qi - kj) * t).astype(F32), False)
        return carry

    lax.fori_loop(0, qi, body, 0)
    tile(qi, 0.0, True)

    for h in range(A_HEADS):
        o = acc_scr[h] / l_scr[h]
        out = o[:, 0:t] - lam * o[:, t:2 * t]
        ms = jnp.mean(out * out, axis=0, keepdims=True)
        out = (out * lax.rsqrt(ms + SUBLN_EPS)) * gt_ref[...] * (1.0 - lam_init)
        o_ref[:, heads[h]] = jnp.transpose(out).astype(BF16)


def _dattn_prompt(proj, lam, subln_g, b, t_seq, lam_init):
    t = min(ATTN_T, t_seq)
    nq = t_seq // t
    r = np.arange(t, dtype=np.float32)
    rc = jnp.asarray(r[None, :] - r[:, None])
    gt = jnp.broadcast_to(subln_g.astype(F32)[:, None], (A_DV, t))
    slopes = [float(s) for s in _alibi_slopes()]
    assert t <= 256
    posk = np.zeros((t, 128), np.float32)
    posk[:, 0], posk[:, 1] = 1.0, r
    posq = np.zeros((A_HEADS, 128, 2 * t), np.float32)
    for h in range(A_HEADS):
        posq[h, 0], posq[h, 1] = -slopes[h] * np.concatenate([r, r]), slopes[h]
    return pl.pallas_call(
        functools.partial(_dattn_prompt_kernel, t=t, lam_init=lam_init, slopes=slopes),
        grid=(b, nq),
        in_specs=[pl.BlockSpec(memory_space=pltpu.SMEM),
                  pl.BlockSpec((t, A_QK), lambda bi, qi: (bi * nq + qi, OFF_AQ // A_QK)),
                  pl.BlockSpec((t_seq, A_QK), lambda bi, qi: (bi, OFF_AK // A_QK)),
                  pl.BlockSpec((t_seq, A_V), lambda bi, qi: (bi, OFF_AV // A_V)),
                  pl.BlockSpec((A_DV, t), lambda bi, qi: (0, 0)),
                  pl.BlockSpec((t, t), lambda bi, qi: (0, 0)),
                  pl.BlockSpec((t, 128), lambda bi, qi: (0, 0)),
                  pl.BlockSpec((A_HEADS, 128, 2 * t), lambda bi, qi: (0, 0, 0))],
        out_specs=pl.BlockSpec((t, A_V), lambda bi, qi: (bi * nq + qi, 0)),
        out_shape=jax.ShapeDtypeStruct((b * t_seq, A_V), BF16),
        scratch_shapes=[pltpu.VMEM((A_HEADS, A_DV, 2 * t), BF16),
                        pltpu.VMEM((A_HEADS, nq, A_DV, t), BF16),
                        pltpu.VMEM((A_HEADS, t, 2 * t), F32),
                        pltpu.VMEM((A_HEADS, 1, 2 * t), F32),
                        pltpu.VMEM((A_HEADS, 1, 2 * t), F32),
                        pltpu.VMEM((A_HEADS, A_DV, 2 * t), F32)],
        compiler_params=_cparams("parallel", "arbitrary"),
        name="dattn_prompt",
    )(lam.reshape(1), proj, proj, proj, gt, rc, jnp.asarray(posk, BF16), jnp.asarray(posq, BF16))


def _pattn_consts(tn, past_len):
    slopes = _alibi_slopes()
    nrow = 2 * A_HEADS * tn
    slope_row = np.array([slopes[(r // tn) % A_HEADS] for r in range(nrow)])[:, None]
    q_row = np.array([r % tn for r in range(nrow)], np.float64)[:, None]
    key = np.arange(PAGE_SIZE, dtype=np.float64)[None, :]
    page0 = slope_row * key - slope_row * (past_len + q_row)
    per_page = np.broadcast_to(slope_row * PAGE_SIZE, (nrow, PAGE_SIZE))
    new = np.where((key < tn) & (key <= q_row), -slope_row * (q_row - key), NEG_BIG)
    return jnp.asarray(page0, F32), jnp.asarray(per_page, F32), jnp.asarray(new, F32)


def _pattn_kernel(pt_ref, lam_ref, qbd_ref, kn_ref, vn_ref, g_ref, page0_ref, perpage_ref, newb_ref, ck_ref, cv_ref,
                  o_ref, kbuf, vbuf, sem, kn_scr, vn_scr, m_scr, l_scr, acc_scr, *, npg, tn, lam_init):
    j = pl.program_id(1)
    nj = pl.num_programs(1)
    step = pl.program_id(0) * nj + j
    slot = step % 2

    def page_copies(st, sl):
        cps = []
        for p_i in range(npg):
            pg = pt_ref[st * npg + p_i]
            cps.append(pltpu.make_async_copy(ck_ref.at[pg], kbuf.at[sl, p_i], sem.at[sl]))
            cps.append(pltpu.make_async_copy(cv_ref.at[pg], vbuf.at[sl, p_i], sem.at[sl]))
        return cps

    @pl.when(step == 0)
    def _():
        for cp in page_copies(0, 0):
            cp.start()

    @pl.when(step + 1 < pl.num_programs(0) * nj)
    def _():
        for cp in page_copies(step + 1, 1 - slot):
            cp.start()

    qbd = qbd_ref[...]
    nrow = qbd.shape[0]
    wide = lambda x: jnp.concatenate([x] * A_HEADS, axis=1)

    @pl.when(step == 0)
    def _():
        kn_scr[...] = jnp.zeros_like(kn_scr)
        vn_scr[...] = jnp.zeros_like(vn_scr)

    @pl.when(j == 0)
    def _():
        kn_scr[0:8, :] = kn_ref[...]
        vn_scr[0:8, :] = vn_ref[...]
        newb = newb_ref[...]
        s = lax.dot_general(qbd, kn_scr[...].astype(BF16), _NT, preferred_element_type=F32)
        s = jnp.where(newb > 0.5 * NEG_BIG, s + newb, NEG_BIG)
        m0 = jnp.broadcast_to(jnp.max(s, axis=1, keepdims=True), s.shape)
        p = jnp.exp(s - m0)
        m_scr[...] = m0
        l_scr[...] = p
        acc_scr[...] = jnp.dot(p.astype(BF16), vn_scr[...].astype(BF16), preferred_element_type=F32)

    for cp in page_copies(step, slot):
        cp.wait()

    page0 = page0_ref[...]
    per_page = perpage_ref[...]
    s_all = []
    for p_i in range(npg):
        page = (j * npg + p_i).astype(F32)
        s = jnp.dot(qbd, kbuf[slot, p_i].astype(BF16), preferred_element_type=F32)
        s_all.append(s + (page0 + per_page * page))
    mx = s_all[0]
    for s in s_all[1:]:
        mx = jnp.maximum(mx, s)
    m_prev = m_scr[...]
    m_new = jnp.maximum(m_prev, jnp.broadcast_to(jnp.max(mx, axis=1, keepdims=True), mx.shape))
    alpha = jnp.exp(m_prev - m_new)
    lsum = jnp.zeros((nrow, PAGE_SIZE), F32)
    pv = [jnp.zeros((nrow, A_DV), F32) for _ in range(A_HEADS)]
    for p_i in range(npg):
        p = jnp.exp(s_all[p_i] - m_new)
        lsum = lsum + p
        pb = p.astype(BF16)
        for h in range(A_HEADS):
            vh = vbuf[slot, p_i, pl.ds(h, PAGE_SIZE, stride=A_HEADS), :].astype(BF16)
            pv[h] = pv[h] + jnp.dot(pb, vh, preferred_element_type=F32)
    m_scr[...] = m_new
    l_scr[...] = alpha * l_scr[...] + lsum
    acc_scr[...] = wide(alpha) * acc_scr[...] + jnp.concatenate(pv, axis=1)

    @pl.when(j == nj - 1)
    def _():
        l_tot = jnp.sum(l_scr[...], axis=1, keepdims=True)
        lam = lam_ref[0]
        g = g_ref[...]
        half = A_HEADS * tn
        for h in range(A_HEADS):
            cols = slice(h * A_DV, (h + 1) * A_DV)
            blk = acc_scr[:, cols] / l_tot
            out = blk[h * tn:(h + 1) * tn] - lam * blk[half + h * tn:half + (h + 1) * tn]
            o_ref[:, cols] = _subln(out, g, lam_init)


def _dattn_sample(qbd, k_new, v_new, cache_k, cache_v, page_table, lam, subln_g, tn, lam_init):
    db, n_pages = page_table.shape
    npg = min(PAGES_PER_STEP, n_pages)
    nrow = 2 * A_HEADS * tn
    assert nrow % 16 == 0 and tn <= 8
    page0, per_page, newb = _pattn_consts(tn, n_pages * PAGE_SIZE)
    const = lambda shape: pl.BlockSpec(shape, lambda bi, j, pt: (0,) * len(shape))
    grid_spec = pltpu.PrefetchScalarGridSpec(
        num_scalar_prefetch=1,
        grid=(db, n_pages // npg),
        in_specs=[pl.BlockSpec(memory_space=pltpu.SMEM),
                  pl.BlockSpec((None, nrow, A_QK), lambda bi, j, pt: (bi, 0, 0)),
                  pl.BlockSpec((None, 8, A_QK), lambda bi, j, pt: (bi, 0, 0)),
                  pl.BlockSpec((None, 8, A_V), lambda bi, j, pt: (bi, 0, 0)),
                  const((1, A_DV)), const((nrow, PAGE_SIZE)), const((nrow, PAGE_SIZE)), const((nrow, PAGE_SIZE)),
                  pl.BlockSpec(memory_space=pl.ANY), pl.BlockSpec(memory_space=pl.ANY)],
        out_specs=pl.BlockSpec((None, tn, A_V), lambda bi, j, pt: (bi, 0, 0)),
        scratch_shapes=[pltpu.VMEM((2, npg, A_QK, PAGE_SIZE), F32),
                        pltpu.VMEM((2, npg, PAGE_SIZE * A_HEADS, A_DV), F32),
                        pltpu.SemaphoreType.DMA((2,)),
                        pltpu.VMEM((PAGE_SIZE, A_QK), F32), pltpu.VMEM((PAGE_SIZE, A_V), F32),
                        pltpu.VMEM((nrow, PAGE_SIZE), F32), pltpu.VMEM((nrow, PAGE_SIZE), F32),
                        pltpu.VMEM((nrow, A_V), F32)],
    )
    return pl.pallas_call(
        functools.partial(_pattn_kernel, npg=npg, tn=tn, lam_init=lam_init),
        grid_spec=grid_spec,
        out_shape=jax.ShapeDtypeStruct((db, tn, A_V), F32),
        compiler_params=_cparams("arbitrary", "arbitrary"),
        name="dattn_sample",
    )(page_table.reshape(-1), lam.reshape(1), qbd, k_new, v_new, subln_g.reshape(1, A_DV), page0, per_page, newb,
      cache_k, cache_v)


ROUTER_ROWS = 8 + N_EXPERTS


def _merge_kernel(x_ref, ret_ref, dif_ref, ga0_ref, ga1_ref, gb0_ref, gb1_ref, wro_ref, wdo_ref, wo_ref, gf_ref,
                  wr_ref, br_ref, h_ref, xn_ref, re_ref, rw_ref):
    a = jnp.dot(ret_ref[...], wro_ref[...], preferred_element_type=F32)
    d = jnp.dot(dif_ref[...], wdo_ref[...], preferred_element_type=F32)
    ga = jnp.concatenate([ga0_ref[...], ga1_ref[...]], axis=1).astype(F32)
    gb = jnp.concatenate([gb0_ref[...], gb1_ref[...]], axis=1).astype(F32)
    m = jax.nn.sigmoid(ga) * a + jax.nn.sigmoid(gb) * d
    h1 = x_ref[...] + jnp.dot(m.astype(BF16), wo_ref[...], preferred_element_type=F32)
    h_ref[...] = h1
    ms = jnp.mean(h1 * h1, axis=-1, keepdims=True)
    xn = h1 * lax.rsqrt(ms + EPS) * gf_ref[...]
    xn_ref[...] = xn

    lt = lax.dot_general(wr_ref[...], xn.astype(BF16), _NT, preferred_element_type=F32) + br_ref[:, 0:1]
    bm = lt.shape[1]
    row8 = lax.broadcasted_iota(jnp.int32, (8, bm), 0)

    def first_argmax(v):
        top = jnp.max(v, axis=0, keepdims=True)
        return top, jnp.min(jnp.where(v == top, row8, 8), axis=0, keepdims=True)

    lg = jnp.where(row8 < N_GROUPS, lt[0:8], NEG_BIG)
    gmax, g_sel = first_argmax(lg)
    g_w = 1.0 / jnp.sum(jnp.exp(lg - gmax), axis=0, keepdims=True)
    el = jnp.zeros((8, bm), F32)
    for g in range(N_GROUPS):
        el = jnp.where(g_sel == g, lt[8 + 8 * g:16 + 8 * g], el)
    ee = jnp.exp(el - jnp.max(el, axis=0, keepdims=True))
    ep = ee / jnp.sum(ee, axis=0, keepdims=True)
    p1, i1 = first_argmax(ep)
    p2, i2 = first_argmax(jnp.where(row8 == i1, -1.0, ep))
    den = p1 + p2
    re_ref[0:1, :] = g_sel * EXP_PER_GROUP + i1
    re_ref[1:2, :] = g_sel * EXP_PER_GROUP + i2
    rw_ref[0:1, :] = g_w * p1 / den
    rw_ref[1:2, :] = g_w * p2 / den


def _merge(x, ret, dif, proj, w_ret_out, w_diff_out, w_out, norm_ffn, w_router, b_router):
    m = x.shape[0]
    bm = min(m, 512)
    const = lambda shape: pl.BlockSpec(shape, lambda i: (0,) * len(shape))
    half = D_MODEL // 2
    gate = lambda off: pl.BlockSpec((bm, half), lambda i: (i, off // half))
    return pl.pallas_call(
        _merge_kernel,
        grid=(m // bm,),
        in_specs=[pl.BlockSpec((bm, D_MODEL), lambda i: (i, 0)),
                  pl.BlockSpec((bm, R_V), lambda i: (i, 0)),
                  pl.BlockSpec((bm, A_V), lambda i: (i, 0)),
                  gate(OFF_GA), gate(OFF_GA + half), gate(OFF_GB), gate(OFF_GB + half),
                  const((R_V, D_MODEL)), const((A_V, D_MODEL)), const((D_MODEL, D_MODEL)),
                  const((1, D_MODEL)), const((ROUTER_ROWS, D_MODEL)), const((ROUTER_ROWS, 128))],
        out_specs=[pl.BlockSpec((bm, D_MODEL), lambda i: (i, 0)),
                   pl.BlockSpec((bm, D_MODEL), lambda i: (i, 0)),
                   pl.BlockSpec((2, bm), lambda i: (0, i)),
                   pl.BlockSpec((2, bm), lambda i: (0, i))],
        out_shape=[jax.ShapeDtypeStruct((m, D_MODEL), F32),
                   jax.ShapeDtypeStruct((m, D_MODEL), F32),
                   jax.ShapeDtypeStruct((2, m), jnp.int32),
                   jax.ShapeDtypeStruct((2, m), F32)],
        compiler_params=_cparams("parallel"),
        name="merge",
    )(x, ret, dif, proj, proj, proj, proj, w_ret_out, w_diff_out, w_out, norm_ffn.reshape(1, D_MODEL),
      w_router, b_router)


def _plan_kernel(re_ref, tri_ref, rank_ref, cnt_ref, carry_scr):
    @pl.when(pl.program_id(0) == 0)
    def _():
        carry_scr[...] = jnp.zeros_like(carry_scr)

    tb = re_ref.shape[1]
    erow = lax.broadcasted_iota(jnp.int32, (N_EXPERTS, tb), 0)
    oh0 = jnp.where(erow == re_ref[0:1, :], 1.0, 0.0)
    oh1 = jnp.where(erow == re_ref[1:2, :], 1.0, 0.0)
    both = oh0 + oh1
    before = jnp.dot(both.astype(BF16), tri_ref[...], preferred_element_type=F32) + carry_scr[:, 0:1]
    rank_ref[0:1, :] = jnp.sum(oh0 * before, axis=0, keepdims=True).astype(jnp.int32)
    rank_ref[1:2, :] = jnp.sum(oh1 * before, axis=0, keepdims=True).astype(jnp.int32)
    carry_scr[...] = carry_scr[...] + jnp.sum(both, axis=1, keepdims=True)
    cnt_ref[...] = carry_scr[...]


def _plan(expert_id):
    m = expert_id.shape[1]
    tb = min(m, 512)
    tri = jnp.asarray(np.triu(np.ones((tb, tb), np.float32), 1), BF16)
    return pl.pallas_call(
        _plan_kernel,
        grid=(m // tb,),
        in_specs=[pl.BlockSpec((2, tb), lambda i: (0, i)),
                  pl.BlockSpec((tb, tb), lambda i: (0, 0))],
        out_specs=[pl.BlockSpec((2, tb), lambda i: (0, i)),
                   pl.BlockSpec((N_EXPERTS, 128), lambda i: (0, 0))],
        out_shape=[jax.ShapeDtypeStruct((2, m), jnp.int32),
                   jax.ShapeDtypeStruct((N_EXPERTS, 128), F32)],
        scratch_shapes=[pltpu.VMEM((N_EXPERTS, 128), F32)],
        compiler_params=_cparams("arbitrary"),
        name="moe_plan",
    )(expert_id, tri)


def _dispatch_kernel(dest_ref, pad_ref, nu_ref, x_ref, xs_ref, zero_scr, sem, zsem):
    tb = x_ref.shape[0]

    def row_copy(t, s):
        return pltpu.make_async_copy(x_ref.at[pl.ds(t, 1)], xs_ref.at[pl.ds(dest_ref[s, t], 1)], sem)

    def issue(t, carry):
        row_copy(t, 0).start()
        row_copy(t, 1).start()
        return carry

    @pl.when(pl.program_id(0) == 0)
    def _():
        zero_scr[...] = jnp.zeros_like(zero_scr)

        def zero_chunk(e, r):
            off = pl.multiple_of(pad_ref[0, e] + r * 8, 8)
            return pltpu.make_async_copy(zero_scr.at[pl.ds(0, 8)], xs_ref.at[pl.ds(off, 8)], zsem)

        def pad_expert(e, carry):
            n = pad_ref[1, e]
            lax.fori_loop(0, n, lambda r, c: (zero_chunk(e, r).start(), c)[1], 0)
            lax.fori_loop(0, n, lambda r, c: (zero_chunk(e, r).wait(), c)[1], 0)
            return carry

        lax.fori_loop(0, N_EXPERTS, pad_expert, 0)
        nblk = xs_ref.shape[0] // MOE_TM
        zero_blk = lambda i: pltpu.make_async_copy(
            zero_scr, xs_ref.at[pl.ds(pl.multiple_of(i * MOE_TM, MOE_TM), MOE_TM)], zsem)
        lax.fori_loop(nu_ref[0], nblk, lambda i, c: (zero_blk(i).start(), c)[1], 0)
        lax.fori_loop(nu_ref[0], nblk, lambda i, c: (zero_blk(i).wait(), c)[1], 0)

    lax.fori_loop(0, tb, issue, 0, unroll=8)

    def drain(t, carry):
        row_copy(t, 0).wait()
        row_copy(t, 1).wait()
        return carry

    lax.fori_loop(0, tb, drain, 0, unroll=8)


def _dispatch(xn, dest, pad_info, n_used, rows):
    m = xn.shape[0]
    tb = min(m, 512)
    return pl.pallas_call(
        _dispatch_kernel,
        grid=(m // tb,),
        in_specs=[pl.BlockSpec((2, tb), lambda i: (0, i), memory_space=pltpu.SMEM),
                  pl.BlockSpec(memory_space=pltpu.SMEM),
                  pl.BlockSpec(memory_space=pltpu.SMEM),
                  pl.BlockSpec((tb, D_MODEL), lambda i: (i, 0))],
        out_specs=pl.BlockSpec(memory_space=pl.ANY),
        out_shape=jax.ShapeDtypeStruct((rows, D_MODEL), F32),
        scratch_shapes=[pltpu.VMEM((MOE_TM, D_MODEL), F32),
                        pltpu.SemaphoreType.DMA(()), pltpu.SemaphoreType.DMA(())],
        compiler_params=_cparams("arbitrary"),
        name="moe_dispatch",
    )(dest, pad_info, n_used, xn)


def _expert_kernel(be_ref, nu_ref, x_ref, wg_ref, wu_ref, wd_ref, y_ref):
    live = pl.program_id(0) < nu_ref[0]

    @pl.when(jnp.logical_not(live))
    def _():
        y_ref[...] = jnp.zeros_like(y_ref)

    @pl.when(live)
    def _():
        x = x_ref[...].astype(BF16)
        gate = jnp.dot(x, wg_ref[...], preferred_element_type=F32)
        up = jnp.dot(x, wu_ref[...], preferred_element_type=F32)
        hid = (gate * jax.nn.sigmoid(gate)) * up
        y_ref[...] = jnp.dot(hid.astype(BF16), wd_ref[...], preferred_element_type=F32)


def _experts(xs, block_e, n_used, w_gate, w_up, w_down):
    rows = xs.shape[0]
    nblk = rows // MOE_TM
    live = lambda i, be, nu: jnp.minimum(i, nu[0] - 1)
    grid_spec = pltpu.PrefetchScalarGridSpec(
        num_scalar_prefetch=2,
        grid=(nblk,),
        in_specs=[pl.BlockSpec((MOE_TM, D_MODEL), lambda i, be, nu: (live(i, be, nu), 0)),
                  pl.BlockSpec((None, D_MODEL, D_EXPERT), lambda i, be, nu: (be[live(i, be, nu)], 0, 0)),
                  pl.BlockSpec((None, D_MODEL, D_EXPERT), lambda i, be, nu: (be[live(i, be, nu)], 0, 0)),
                  pl.BlockSpec((None, D_EXPERT, D_MODEL), lambda i, be, nu: (be[live(i, be, nu)], 0, 0))],
        out_specs=pl.BlockSpec((MOE_TM, D_MODEL), lambda i, be, nu: (i, 0)),
    )
    return pl.pallas_call(
        _expert_kernel,
        grid_spec=grid_spec,
        out_shape=jax.ShapeDtypeStruct((rows, D_MODEL), F32),
        compiler_params=_cparams("arbitrary"),
        name="moe_experts",
    )(block_e, n_used, xs, w_gate, w_up, w_down)


def _combine_kernel(dest_ref, dnext_ref, h_ref, w_ref, gf_ref, yb_ref, y_ref, g_scr, sem):
    tb = h_ref.shape[0]
    i = pl.program_id(0)
    slot = i % 2

    def row_copy(d_ref, sl, t, s):
        return pltpu.make_async_copy(yb_ref.at[pl.ds(d_ref[s, t], 1)], g_scr.at[sl, s, pl.ds(t, 1)], sem.at[sl])

    def gather(d_ref, sl, wait):
        def body(t, carry):
            for s in range(TOP_K):
                cp = row_copy(d_ref, sl, t, s)
                cp.wait() if wait else cp.start()
            return carry

        lax.fori_loop(0, tb, body, 0, unroll=8)

    @pl.when(i == 0)
    def _():
        gather(dest_ref, 0, False)

    @pl.when(i + 1 < pl.num_programs(0))
    def _():
        gather(dnext_ref, 1 - slot, False)

    gather(dest_ref, slot, True)
    y = h_ref[...] + (w_ref[:, 0:1] * g_scr[slot, 0] + w_ref[:, 1:2] * g_scr[slot, 1])
    ms = jnp.mean(y * y, axis=-1, keepdims=True)
    y_ref[...] = y * lax.rsqrt(ms + EPS) * gf_ref[...]


def _combine(h1, yb, dest, weights_t, norm_final):
    m = h1.shape[0]
    tb = min(m, 256)
    last = m // tb - 1
    return pl.pallas_call(
        _combine_kernel,
        grid=(m // tb,),
        in_specs=[pl.BlockSpec((2, tb), lambda i: (0, i), memory_space=pltpu.SMEM),
                  pl.BlockSpec((2, tb), lambda i: (0, jnp.minimum(i + 1, last)), memory_space=pltpu.SMEM),
                  pl.BlockSpec((tb, D_MODEL), lambda i: (i, 0)),
                  pl.BlockSpec((tb, 2), lambda i: (i, 0)),
                  pl.BlockSpec((1, D_MODEL), lambda i: (0, 0)),
                  pl.BlockSpec(memory_space=pl.ANY)],
        out_specs=pl.BlockSpec((tb, D_MODEL), lambda i: (i, 0)),
        out_shape=jax.ShapeDtypeStruct((m, D_MODEL), F32),
        scratch_shapes=[pltpu.VMEM((2, TOP_K, tb, D_MODEL), F32), pltpu.SemaphoreType.DMA((2,))],
        compiler_params=_cparams("arbitrary"),
        name="moe_combine",
    )(dest, dest, h1, weights_t, norm_final.reshape(1, D_MODEL), yb)


def _moe_and_final(h1, xn, expert_id, weights, w_gate, w_up, w_down, norm_final):
    m = h1.shape[0]
    rank, counts = _plan(expert_id)
    counts = counts[:, 0].astype(jnp.int32)
    padded = (counts + MOE_TM - 1) // MOE_TM * MOE_TM
    pad_end = jnp.cumsum(padded)
    pad_start = pad_end - padded
    is_e = expert_id[:, :, None] == jnp.arange(N_EXPERTS, dtype=jnp.int32)
    dest = jnp.sum(jnp.where(is_e, pad_start, 0), axis=-1) + rank
    nblk = (m * TOP_K) // MOE_TM + N_EXPERTS
    block_start = jnp.arange(nblk, dtype=jnp.int32) * MOE_TM
    block_e = jnp.minimum(jnp.sum((block_start[:, None] >= pad_end[None, :]).astype(jnp.int32), axis=1),
                          N_EXPERTS - 1).astype(jnp.int32)
    n_used = (pad_end[-1:] // MOE_TM).astype(jnp.int32)
    zero_from = (pad_start + counts) // 8 * 8
    pad_info = jnp.stack([zero_from, (pad_end - zero_from) // 8]).astype(jnp.int32)
    xs = _dispatch(xn, dest, pad_info, n_used, nblk * MOE_TM)
    yb = _experts(xs, block_e, n_used, w_gate, w_up, w_down)
    return _combine(h1, yb, dest, weights.T, norm_final)


def kernel(x_prompt, x_sample, state_ret, cache_k, cache_v, page_table, norm_mix, w_in, lambda_q1, lambda_k1,
           lambda_q2, lambda_k2, subln_g, w_ret_out, w_diff_out, w_out, norm_ffn, w_group, b_group, w_expert,
           b_expert, w_up, w_gate, w_down, norm_final):
    b, t_seq, _ = x_prompt.shape
    db, tn, _ = x_sample.shape
    depth = w_in.shape[0]
    assert depth == 1 and tn <= 8
    n_phys = cache_k.shape[1]
    l = 0
    lam_init = 0.8 - 0.6 * float(np.exp(-0.3 * l))
    lam = (jnp.exp(jnp.sum(lambda_q1[l].astype(F32) * lambda_k1[l].astype(F32)))
           - jnp.exp(jnp.sum(lambda_q2[l].astype(F32) * lambda_k2[l].astype(F32))) + lam_init)

    w_in_b = w_in[l].astype(BF16)
    wkt_b = w_in[l][:, OFF_AK:OFF_AK + A_QK].T.astype(BF16)
    wro, wdo, wo = w_ret_out[l].astype(BF16), w_diff_out[l].astype(BF16), w_out[l].astype(BF16)
    wg_b, wu_b, wd_b = w_gate[l].astype(BF16), w_up[l].astype(BF16), w_down[l].astype(BF16)
    w_router = jnp.zeros((ROUTER_ROWS, D_MODEL), F32)
    w_router = w_router.at[0:N_GROUPS].set(w_group[l].T).at[8:].set(w_expert[l].T).astype(BF16)
    b_router = jnp.zeros((ROUTER_ROWS,), F32).at[0:N_GROUPS].set(b_group[l]).at[8:].set(b_expert[l])
    b_router = jnp.broadcast_to(b_router[:, None], (ROUTER_ROWS, 128))

    def tail(x, ret, dif, proj):
        h1, xn, eid, wts = _merge(x, ret, dif, proj, wro, wdo, wo, norm_ffn[l], w_router, b_router)
        return _moe_and_final(h1, xn, eid, wts, wg_b, wu_b, wd_b, norm_final)

    xp = x_prompt.reshape(b * t_seq, D_MODEL)
    proj_p, kt_p, vr_p = _proj(xp, norm_mix[l], w_in_b, wkt_b, seq_len=t_seq)
    ret_p, s_p = _ret_prompt(proj_p, b, t_seq)
    dif_p = _dattn_prompt(proj_p, lam, subln_g[l], b, t_seq, lam_init)
    y_p = tail(xp, ret_p, dif_p, proj_p)

    xs = x_sample.reshape(db * tn, D_MODEL)
    proj_s, k_s, v_s = _proj(xs, norm_mix[l], w_in_b, wkt_b)
    ret_s, s_s = _ret_sample(proj_s, state_ret[l], tn)
    aq = proj_s[:, OFF_AQ:OFF_AQ + A_QK].astype(F32).reshape(db, tn, A_HEADS, 2, A_DH) * (A_DH ** -0.5)
    eye_h = jnp.eye(A_HEADS, dtype=F32)
    eye_c = jnp.eye(2, dtype=F32)
    qbd = jnp.einsum('bqhcd,hH,cC->bCHqhcd', aq, eye_h, eye_c).reshape(db, 2 * A_HEADS * tn, A_QK).astype(BF16)
    pad_new = lambda a: jnp.pad(a.reshape(db, tn, -1), ((0, 0), (0, 8 - tn), (0, 0)))
    ck_pages = jnp.transpose(cache_k[l], (0, 2, 3, 4, 1)).reshape(n_phys, A_QK, PAGE_SIZE)
    cv_pages = cache_v[l].reshape(n_phys, PAGE_SIZE * A_HEADS, A_DV)
    dif_s = _dattn_sample(qbd, pad_new(k_s), pad_new(v_s), ck_pages, cv_pages,
                          page_table.astype(jnp.int32), lam, subln_g[l], tn, lam_init)
    y_s = tail(xs, ret_s, dif_s.reshape(db * tn, A_V).astype(BF16), proj_s)

    return (y_p.reshape(b, t_seq, D_MODEL),
            y_s.reshape(db, tn, D_MODEL),
            s_p[None],
            s_s[None],
            jnp.transpose(kt_p.reshape(1, b, A_HEADS, 2, A_DH, t_seq), (0, 1, 5, 2, 3, 4)),
            vr_p.reshape(1, b, t_seq, A_HEADS, A_DV),
            k_s.reshape(1, db, tn, A_HEADS, 2, A_DH),
            v_s.reshape(1, db, tn, A_HEADS, A_DV))
```

```python
import functools

import numpy as np
import jax
import jax.numpy as jnp
from jax import lax
from jax.experimental import pallas as pl
from jax.experimental.pallas import tpu as pltpu

D_MODEL = 1024
R_HEADS, R_DK, R_DV = 4, 128, 256
R_CHUNK = 256
A_HEADS, A_DH, A_DV = 4, 64, 128
N_GROUPS, EXP_PER_GROUP, TOP_K, D_EXPERT = 4, 8, 2, 512
N_EXPERTS = N_GROUPS * EXP_PER_GROUP
PAGE_SIZE = 128
EPS, SUBLN_EPS, NEG_BIG = 1e-6, 1e-5, -1e30

R_QK, R_V = R_HEADS * R_DK, R_HEADS * R_DV
A_QK, A_V = A_HEADS * 2 * A_DH, A_HEADS * A_DV
OFF_RQ, OFF_RK, OFF_RV, OFF_RG = 0, R_QK, 2 * R_QK, 2 * R_QK + R_V
OFF_AQ = OFF_RG + R_V
OFF_AK, OFF_AV = OFF_AQ + A_QK, OFF_AQ + 2 * A_QK
OFF_GA = OFF_AV + A_V
OFF_GB = OFF_GA + D_MODEL
N_IN = OFF_GB + D_MODEL

PROJ_BM = 512
PROJ_BN = 512
ATTN_T = 256
PAGES_PER_STEP = 16
MOE_TM = 256
VMEM_LIMIT = 56 * 1024 * 1024

F32, BF16 = jnp.float32, jnp.bfloat16
_NT = (((1,), (1,)), ((), ()))
_TN = (((0,), (0,)), ((), ()))


def _cparams(*sem):
    return pltpu.CompilerParams(dimension_semantics=sem, vmem_limit_bytes=VMEM_LIMIT)


def _log_decay():
    return np.log1p(-np.exp2(-5.0 - np.arange(R_HEADS, dtype=np.float64)))


def _alibi_slopes():
    return np.exp2(-8.0 * np.arange(1, A_HEADS + 1, dtype=np.float64) / A_HEADS)


def _proj_kernel(x_ref, g_ref, w_ref, wkt_ref, p_ref, k_ref, v_ref, xn_ref, *, seq_major):
    x = x_ref[...]
    ms = jnp.mean(x * x, axis=-1, keepdims=True)
    xn_ref[...] = (x * lax.rsqrt(ms + EPS) * g_ref[...]).astype(BF16)
    for j in range(N_IN // PROJ_BN):
        cols = slice(j * PROJ_BN, (j + 1) * PROJ_BN)
        acc = jnp.dot(xn_ref[...], w_ref[:, cols], preferred_element_type=F32)
        p_ref[:, cols] = acc.astype(BF16)
        if j == OFF_AK // PROJ_BN:
            if seq_major:
                k_ref[...] = lax.dot_general(wkt_ref[...], xn_ref[...], _NT, preferred_element_type=F32)
            else:
                k_ref[...] = acc
        if j == OFF_AV // PROJ_BN:
            if seq_major:
                for h in range(A_HEADS):
                    v_ref[pl.ds(h, acc.shape[0], stride=A_HEADS), :] = acc[:, h * A_DV:(h + 1) * A_DV]
            else:
                v_ref[...] = acc


def _proj(x, gain, w_bf16, wkt_bf16, seq_len=None):
    m = x.shape[0]
    bm = min(m, PROJ_BM) if seq_len is None else min(seq_len, PROJ_BM)
    if seq_len is None:
        kv_specs = [pl.BlockSpec((bm, A_QK), lambda i: (i, 0)),
                    pl.BlockSpec((bm, A_V), lambda i: (i, 0))]
        kv_shapes = [jax.ShapeDtypeStruct((m, A_QK), F32), jax.ShapeDtypeStruct((m, A_V), F32)]
    else:
        per_seq = seq_len // bm
        kv_specs = [pl.BlockSpec((None, A_QK, bm), lambda i: (i // per_seq, 0, i % per_seq)),
                    pl.BlockSpec((bm * A_HEADS, A_DV), lambda i: (i, 0))]
        kv_shapes = [jax.ShapeDtypeStruct((m // seq_len, A_QK, seq_len), F32),
                     jax.ShapeDtypeStruct((m * A_HEADS, A_DV), F32)]
    resident = lambda shape: pl.BlockSpec(shape, lambda i: (0, 0), pipeline_mode=pl.Buffered(1))
    return pl.pallas_call(
        functools.partial(_proj_kernel, seq_major=seq_len is not None),
        grid=(m // bm,),
        in_specs=[pl.BlockSpec((bm, D_MODEL), lambda i: (i, 0)),
                  resident((1, D_MODEL)), resident((D_MODEL, N_IN)), resident((A_QK, D_MODEL))],
        out_specs=[pl.BlockSpec((bm, N_IN), lambda i: (i, 0))] + kv_specs,
        out_shape=[jax.ShapeDtypeStruct((m, N_IN), BF16)] + kv_shapes,
        scratch_shapes=[pltpu.VMEM((bm, D_MODEL), BF16)],
        compiler_params=_cparams("parallel"),
        name="proj",
    )(x, gain.reshape(1, D_MODEL), w_bf16, wkt_bf16)


def _ret_consts(c, rows_per_seq):
    lg = _log_decay()
    pos = np.arange(c, dtype=np.float64)
    dist = pos[:, None] - pos[None, :]
    n = c * rows_per_seq
    intra = np.zeros((R_HEADS, n, n), np.float32)
    qdec = np.zeros((n, R_QK), np.float32)
    kdec = np.zeros((n, R_QK), np.float32)
    for h in range(R_HEADS):
        blk = np.where(dist >= 0, np.exp(np.maximum(dist, 0.0) * lg[h]), 0.0)
        for s in range(rows_per_seq):
            intra[h, s * c:(s + 1) * c, s * c:(s + 1) * c] = blk
            qdec[s * c:(s + 1) * c, h * R_DK:(h + 1) * R_DK] = np.exp((pos + 1.0) * lg[h])[:, None]
            kdec[s * c:(s + 1) * c, h * R_DK:(h + 1) * R_DK] = (
                np.exp((c - 1.0 - pos) * lg[h])[:, None] * R_DK ** -0.5)
    cdec = [float(np.float32(np.exp(c * lg[h]))) for h in range(R_HEADS)]
    return jnp.asarray(intra), jnp.asarray(qdec), jnp.asarray(kdec), cdec


def _gated_norm(o, g):
    ms = jnp.mean(o * o, axis=-1, keepdims=True)
    return (o * lax.rsqrt(ms + EPS)) * (g * jax.nn.sigmoid(g))


def _ret_prompt_kernel(q_ref, k_ref, v_ref, rg_ref, intra_ref, qdec_ref, kdec_ref,
                       ret_ref, sout_ref, s_scr, *, cdec):
    c = pl.program_id(1)

    @pl.when(c == 0)
    def _():
        s_scr[...] = jnp.zeros_like(s_scr)

    q = q_ref[...]
    kf = k_ref[...].astype(F32)
    qd = (q.astype(F32) * qdec_ref[...]).astype(BF16)
    kd = (kf * kdec_ref[...]).astype(BF16)
    ks = (kf * (R_DK ** -0.5)).astype(BF16)
    for h in range(R_HEADS):
        hk = slice(h * R_DK, (h + 1) * R_DK)
        hv = slice(h * R_DV, (h + 1) * R_DV)
        vh = v_ref[:, hv]
        s_old = s_scr[h]
        sc = lax.dot_general(q[:, hk], ks[:, hk], _NT, preferred_element_type=F32) * intra_ref[h]
        o = (jnp.dot(sc.astype(BF16), vh, preferred_element_type=F32)
             + jnp.dot(qd[:, hk], s_old.astype(BF16), preferred_element_type=F32))
        s_scr[h] = cdec[h] * s_old + lax.dot_general(kd[:, hk], vh, _TN, preferred_element_type=F32)
        ret_ref[:, hv] = _gated_norm(o, rg_ref[:, hv].astype(F32)).astype(BF16)

    @pl.when(c == pl.num_programs(1) - 1)
    def _():
        sout_ref[0] = s_scr[...]


def _ret_prompt(proj, b, t):
    c = min(R_CHUNK, t)
    nc = t // c
    intra, qdec, kdec, cdec = _ret_consts(c, 1)
    row = lambda bi, ci: bi * nc + ci
    return pl.pallas_call(
        functools.partial(_ret_prompt_kernel, cdec=cdec),
        grid=(b, nc),
        in_specs=[pl.BlockSpec((c, R_QK), lambda bi, ci: (row(bi, ci), OFF_RQ // R_QK)),
                  pl.BlockSpec((c, R_QK), lambda bi, ci: (row(bi, ci), OFF_RK // R_QK)),
                  pl.BlockSpec((c, R_V), lambda bi, ci: (row(bi, ci), OFF_RV // R_V)),
                  pl.BlockSpec((c, R_V), lambda bi, ci: (row(bi, ci), OFF_RG // R_V)),
                  pl.BlockSpec((R_HEADS, c, c), lambda bi, ci: (0, 0, 0)),
                  pl.BlockSpec((c, R_QK), lambda bi, ci: (0, 0)),
                  pl.BlockSpec((c, R_QK), lambda bi, ci: (0, 0))],
        out_specs=[pl.BlockSpec((c, R_V), lambda bi, ci: (row(bi, ci), 0)),
                   pl.BlockSpec((1, R_HEADS, R_DK, R_DV), lambda bi, ci: (bi, 0, 0, 0))],
        out_shape=[jax.ShapeDtypeStruct((b * t, R_V), BF16),
                   jax.ShapeDtypeStruct((b, R_HEADS, R_DK, R_DV), F32)],
        scratch_shapes=[pltpu.VMEM((R_HEADS, R_DK, R_DV), F32)],
        compiler_params=_cparams("parallel", "arbitrary"),
        name="ret_prompt",
    )(proj, proj, proj, proj, intra, qdec, kdec)


def _ret_sample_kernel(q_ref, k_ref, v_ref, rg_ref, s_ref, intra_ref, qdec_ref, kdec_ref,
                       ret_ref, sout_ref, *, cdec, tn, nb):
    q = q_ref[...]
    kf = k_ref[...].astype(F32)
    qd = (q.astype(F32) * qdec_ref[...]).astype(BF16)
    kdf = kf * kdec_ref[...]
    ks = (kf * (R_DK ** -0.5)).astype(BF16)
    rows = lax.broadcasted_iota(jnp.int32, (nb * tn, 1), 0)
    for h in range(R_HEADS):
        hk = slice(h * R_DK, (h + 1) * R_DK)
        hv = slice(h * R_DV, (h + 1) * R_DV)
        vh = v_ref[:, hv]
        sc = lax.dot_general(q[:, hk], ks[:, hk], _NT, preferred_element_type=F32) * intra_ref[h]
        o = jnp.dot(sc.astype(BF16), vh, preferred_element_type=F32)
        for s in range(nb):
            mine = (rows >= s * tn) & (rows < (s + 1) * tn)
            s_old = s_ref[s, h]
            o = o + jnp.where(mine, jnp.dot(qd[:, hk], s_old.astype(BF16), preferred_element_type=F32), 0.0)
            kd_s = jnp.where(mine, kdf[:, hk], 0.0).astype(BF16)
            sout_ref[s, h] = cdec[h] * s_old + lax.dot_general(kd_s, vh, _TN, preferred_element_type=F32)
        ret_ref[:, hv] = _gated_norm(o, rg_ref[:, hv].astype(F32)).astype(BF16)


def _ret_sample(proj, state, tn):
    db = state.shape[0]
    nb = min(db, 8)
    n = nb * tn
    intra, qdec, kdec, cdec = _ret_consts(tn, nb)
    return pl.pallas_call(
        functools.partial(_ret_sample_kernel, cdec=cdec, tn=tn, nb=nb),
        grid=(db // nb,),
        in_specs=[pl.BlockSpec((n, R_QK), lambda i: (i, OFF_RQ // R_QK)),
                  pl.BlockSpec((n, R_QK), lambda i: (i, OFF_RK // R_QK)),
                  pl.BlockSpec((n, R_V), lambda i: (i, OFF_RV // R_V)),
                  pl.BlockSpec((n, R_V), lambda i: (i, OFF_RG // R_V)),
                  pl.BlockSpec((nb, R_HEADS, R_DK, R_DV), lambda i: (i, 0, 0, 0)),
                  pl.BlockSpec((R_HEADS, n, n), lambda i: (0, 0, 0)),
                  pl.BlockSpec((n, R_QK), lambda i: (0, 0)),
                  pl.BlockSpec((n, R_QK), lambda i: (0, 0))],
        out_specs=[pl.BlockSpec((n, R_V), lambda i: (i, 0)),
                   pl.BlockSpec((nb, R_HEADS, R_DK, R_DV), lambda i: (i, 0, 0, 0))],
        out_shape=[jax.ShapeDtypeStruct((db * tn, R_V), BF16),
                   jax.ShapeDtypeStruct(state.shape, F32)],
        compiler_params=_cparams("parallel"),
        name="ret_sample",
    )(proj, proj, proj, proj, state, intra, qdec, kdec)


def _subln(o, g, lam_init):
    ms = jnp.mean(o * o, axis=-1, keepdims=True)
    return (o * lax.rsqrt(ms + SUBLN_EPS)) * g * (1.0 - lam_init)


def _dattn_prompt_kernel(lam_ref, q_ref, k_ref, v_ref, gt_ref, rc_ref, posk_ref, posq_ref, o_ref,
                         qq_scr, vt_scr, s_scr, m_scr, l_scr, acc_scr, *, t, lam_init, slopes):
    qi = pl.program_id(1)
    lam = lam_ref[0]
    heads = [slice(h * A_DV, (h + 1) * A_DV) for h in range(A_HEADS)]

    @pl.when(qi == 0)
    def _():
        for h in range(A_HEADS):
            for kj in range(v_ref.shape[0] // t):
                vt_scr[h, kj] = jnp.transpose(v_ref[kj * t:(kj + 1) * t, heads[h]].astype(F32)).astype(BF16)

    rc = rc_ref[...]
    for h in range(A_HEADS):
        qt = jnp.transpose(q_ref[:, heads[h]].astype(F32) * (A_DH ** -0.5))
        feat = lax.broadcasted_iota(jnp.int32, qt.shape, 0)
        qq_scr[h, :, 0:t] = jnp.where(feat < A_DH, qt, 0.0).astype(BF16)
        qq_scr[h, :, t:2 * t] = jnp.where(feat >= A_DH, qt, 0.0).astype(BF16)
    m_scr[...] = jnp.full_like(m_scr, NEG_BIG)
    l_scr[...] = jnp.zeros_like(l_scr)
    acc_scr[...] = jnp.zeros_like(acc_scr)

    def tile(kj, dist, masked):
        rows = pl.ds(pl.multiple_of(kj * t, t), t)
        posk = posk_ref[...]
        for h in range(A_HEADS):
            s_scr[h] = jnp.dot(jnp.concatenate([k_ref[rows, heads[h]], posk], axis=1),
                               jnp.concatenate([qq_scr[h], posq_ref[h]], axis=0), preferred_element_type=F32)
        p_all, alpha_all = [], []
        for h in range(A_HEADS):
            s = s_scr[h]
            if masked:
                s = jnp.where(jnp.concatenate([rc, rc], axis=1) >= 0.0, s, NEG_BIG)
            off = slopes[h] * dist
            m_prev = m_scr[h]
            m_new = jnp.maximum(m_prev, jnp.max(s, axis=0, keepdims=True) - off)
            alpha = jnp.exp(m_prev - m_new)
            p = jnp.exp(s - (m_new + off))
            l_scr[h] = alpha * l_scr[h] + jnp.sum(p, axis=0, keepdims=True)
            m_scr[h] = m_new
            p_all.append(p.astype(BF16))
            alpha_all.append(alpha)
        for h in range(A_HEADS):
            acc_scr[h] = alpha_all[h] * acc_scr[h] + jnp.dot(vt_scr[h, kj], p_all[h], preferred_element_type=F32)

    def body(kj, carry):
        tile(kj, ((qi - kj) * t).astype(F32), False)
        return carry

    lax.fori_loop(0, qi, body, 0)
    tile(qi, 0.0, True)

    for h in range(A_HEADS):
        o = acc_scr[h] / l_scr[h]
        out = o[:, 0:t] - lam * o[:, t:2 * t]
        ms = jnp.mean(out * out, axis=0, keepdims=True)
        out = (out * lax.rsqrt(ms + SUBLN_EPS)) * gt_ref[...] * (1.0 - lam_init)
        o_ref[:, heads[h]] = jnp.transpose(out).astype(BF16)


def _dattn_prompt(proj, lam, subln_g, b, t_seq, lam_init):
    t = min(ATTN_T, t_seq)
    nq = t_seq // t
    r = np.arange(t, dtype=np.float32)
    rc = jnp.asarray(r[None, :] - r[:, None])
    gt = jnp.broadcast_to(subln_g.astype(F32)[:, None], (A_DV, t))
    slopes = [float(s) for s in _alibi_slopes()]
    assert t <= 256
    posk = np.zeros((t, 128), np.float32)
    posk[:, 0], posk[:, 1] = 1.0, r
    posq = np.zeros((A_HEADS, 128, 2 * t), np.float32)
    for h in range(A_HEADS):
        posq[h, 0], posq[h, 1] = -slopes[h] * np.concatenate([r, r]), slopes[h]
    return pl.pallas_call(
        functools.partial(_dattn_prompt_kernel, t=t, lam_init=lam_init, slopes=slopes),
        grid=(b, nq),
        in_specs=[pl.BlockSpec(memory_space=pltpu.SMEM),
                  pl.BlockSpec((t, A_QK), lambda bi, qi: (bi * nq + qi, OFF_AQ // A_QK)),
                  pl.BlockSpec((t_seq, A_QK), lambda bi, qi: (bi, OFF_AK // A_QK)),
                  pl.BlockSpec((t_seq, A_V), lambda bi, qi: (bi, OFF_AV // A_V)),
                  pl.BlockSpec((A_DV, t), lambda bi, qi: (0, 0)),
                  pl.BlockSpec((t, t), lambda bi, qi: (0, 0)),
                  pl.BlockSpec((t, 128), lambda bi, qi: (0, 0)),
                  pl.BlockSpec((A_HEADS, 128, 2 * t), lambda bi, qi: (0, 0, 0))],
        out_specs=pl.BlockSpec((t, A_V), lambda bi, qi: (bi * nq + qi, 0)),
        out_shape=jax.ShapeDtypeStruct((b * t_seq, A_V), BF16),
        scratch_shapes=[pltpu.VMEM((A_HEADS, A_DV, 2 * t), BF16),
                        pltpu.VMEM((A_HEADS, nq, A_DV, t), BF16),
                        pltpu.VMEM((A_HEADS, t, 2 * t), F32),
                        pltpu.VMEM((A_HEADS, 1, 2 * t), F32),
                        pltpu.VMEM((A_HEADS, 1, 2 * t), F32),
                        pltpu.VMEM((A_HEADS, A_DV, 2 * t), F32)],
        compiler_params=_cparams("parallel", "arbitrary"),
        name="dattn_prompt",
    )(lam.reshape(1), proj, proj, proj, gt, rc, jnp.asarray(posk, BF16), jnp.asarray(posq, BF16))


def _pattn_consts(tn, past_len):
    slopes = _alibi_slopes()
    nrow = 2 * A_HEADS * tn
    slope_row = np.array([slopes[(r // tn) % A_HEADS] for r in range(nrow)])[:, None]
    q_row = np.array([r % tn for r in range(nrow)], np.float64)[:, None]
    key = np.arange(PAGE_SIZE, dtype=np.float64)[None, :]
    page0 = slope_row * key - slope_row * (past_len + q_row)
    per_page = np.broadcast_to(slope_row * PAGE_SIZE, (nrow, PAGE_SIZE))
    new = np.where((key < tn) & (key <= q_row), -slope_row * (q_row - key), NEG_BIG)
    return jnp.asarray(page0, F32), jnp.asarray(per_page, F32), jnp.asarray(new, F32)


N_PATTN_IN, N_PATTN_SCRATCH = 10, 5


def _pattn_row(pt_ref, lam_ref, qbd_ref, kn_ref, vn_ref, g_ref, page0_ref, perpage_ref, newb_ref, ck_ref, cv_ref,
               o_ref, kbuf, vbuf, sem, kn_scr, vn_scr, *, npg, nj, tn, lam_init, side_work=()):
    row = pl.program_id(0)
    n_groups = pl.num_programs(0) * nj

    def page_copies(st, sl):
        cps = []
        for p_i in range(npg):
            pg = pt_ref[st * npg + p_i]
            cps.append(pltpu.make_async_copy(ck_ref.at[pg], kbuf.at[sl, p_i], sem.at[sl]))
            cps.append(pltpu.make_async_copy(cv_ref.at[pg], vbuf.at[sl, p_i], sem.at[sl]))
        return cps

    @pl.when(row == 0)
    def _():
        kn_scr[...] = jnp.zeros_like(kn_scr)
        vn_scr[...] = jnp.zeros_like(vn_scr)
        for cp in page_copies(0, 0):
            cp.start()

    qbd = qbd_ref[...]
    nrow = qbd.shape[0]
    wide = lambda x: jnp.concatenate([x] * A_HEADS, axis=1)

    kn_scr[0:8, :] = kn_ref[...]
    vn_scr[0:8, :] = vn_ref[...]
    newb = newb_ref[...]
    s = lax.dot_general(qbd, kn_scr[...].astype(BF16), _NT, preferred_element_type=F32)
    s = jnp.where(newb > 0.5 * NEG_BIG, s + newb, NEG_BIG)
    m_run = jnp.broadcast_to(jnp.max(s, axis=1, keepdims=True), s.shape)
    l_run = jnp.exp(s - m_run)
    acc = jnp.dot(l_run.astype(BF16), vn_scr[...].astype(BF16), preferred_element_type=F32)

    page0 = page0_ref[...]
    per_page = perpage_ref[...]
    for j in range(nj):
        group = row * nj + j
        slot = j % 2 if nj % 2 == 0 else group % 2

        @pl.when(group + 1 < n_groups)
        def _():
            for cp in page_copies(group + 1, 1 - slot):
                cp.start()

        for cp in page_copies(group, slot):
            cp.wait()
        for stage in side_work[j * len(side_work) // nj:(j + 1) * len(side_work) // nj]:
            stage()

        s_all = []
        for p_i in range(npg):
            s = jnp.dot(qbd, kbuf[slot, p_i].astype(BF16), preferred_element_type=F32)
            s_all.append(s + (page0 + per_page * float(j * npg + p_i)))
        mx = s_all[0]
        for s in s_all[1:]:
            mx = jnp.maximum(mx, s)
        m_new = jnp.maximum(m_run, jnp.broadcast_to(jnp.max(mx, axis=1, keepdims=True), mx.shape))
        alpha = jnp.exp(m_run - m_new)
        lsum = jnp.zeros((nrow, PAGE_SIZE), F32)
        pv = [jnp.zeros((nrow, A_DV), F32) for _ in range(A_HEADS)]
        for p_i in range(npg):
            p = jnp.exp(s_all[p_i] - m_new)
            lsum = lsum + p
            pb = p.astype(BF16)
            for h in range(A_HEADS):
                vh = vbuf[slot, p_i, pl.ds(h, PAGE_SIZE, stride=A_HEADS), :].astype(BF16)
                pv[h] = pv[h] + jnp.dot(pb, vh, preferred_element_type=F32)
        m_run = m_new
        l_run = alpha * l_run + lsum
        acc = wide(alpha) * acc + jnp.concatenate(pv, axis=1)

    l_tot = jnp.sum(l_run, axis=1, keepdims=True)
    lam = lam_ref[0]
    g = g_ref[...]
    half = A_HEADS * tn
    for h in range(A_HEADS):
        blk = acc[:, h * A_DV:(h + 1) * A_DV] / l_tot
        out = blk[h * tn:(h + 1) * tn] - lam * blk[half + h * tn:half + (h + 1) * tn]
        o_ref[:, h * A_DV:(h + 1) * A_DV] = _subln(out, g, lam_init)


def _pattn_kernel(*refs, **static):
    _pattn_row(*refs, **static)


def _merge_pattn_kernel(pt_ref, *refs, **static):
    merge_in, rest = refs[:N_MERGE_IN], refs[N_MERGE_IN:]
    pattn_in, rest = rest[:N_PATTN_IN], rest[N_PATTN_IN:]
    merge_out, o_ref, scratch = rest[:N_MERGE_OUT], rest[N_MERGE_OUT], rest[N_MERGE_OUT + 1:]
    _pattn_row(pt_ref, *pattn_in, o_ref, *scratch, side_work=_merge_stages(*merge_in, *merge_out), **static)


def _pattn_specs(db, n_pages, tn, wrap):
    npg = min(PAGES_PER_STEP, n_pages)
    nrow = 2 * A_HEADS * tn
    assert nrow % 16 == 0 and tn <= 8 and n_pages % npg == 0
    const = lambda shape: pl.BlockSpec(shape, wrap(lambda i: (0,) * len(shape)))
    per_row = lambda r, c: pl.BlockSpec((None, r, c), wrap(lambda i: (i, 0, 0)))
    in_specs = [pl.BlockSpec(memory_space=pltpu.SMEM), per_row(nrow, A_QK), per_row(8, A_QK), per_row(8, A_V),
                const((1, A_DV)), const((nrow, PAGE_SIZE)), const((nrow, PAGE_SIZE)), const((nrow, PAGE_SIZE)),
                pl.BlockSpec(memory_space=pl.ANY), pl.BlockSpec(memory_space=pl.ANY)]
    out_spec = per_row(tn, A_V)
    scratch = [pltpu.VMEM((2, npg, A_QK, PAGE_SIZE), F32),
               pltpu.VMEM((2, npg, PAGE_SIZE * A_HEADS, A_DV), F32),
               pltpu.SemaphoreType.DMA((2,)),
               pltpu.VMEM((PAGE_SIZE, A_QK), F32), pltpu.VMEM((PAGE_SIZE, A_V), F32)]
    static = dict(npg=npg, nj=n_pages // npg, tn=tn)
    return in_specs, out_spec, jax.ShapeDtypeStruct((db, tn, A_V), F32), scratch, static


def _dattn_sample(qbd, k_new, v_new, cache_k, cache_v, page_table, lam, subln_g, tn, lam_init):
    db, n_pages = page_table.shape
    in_specs, out_spec, out_shape, scratch, static = _pattn_specs(db, n_pages, tn, lambda f: lambda i, pt: f(i))
    grid_spec = pltpu.PrefetchScalarGridSpec(num_scalar_prefetch=1, grid=(db,), in_specs=in_specs,
                                             out_specs=out_spec, scratch_shapes=scratch)
    return pl.pallas_call(
        functools.partial(_pattn_kernel, lam_init=lam_init, **static),
        grid_spec=grid_spec,
        out_shape=out_shape,
        compiler_params=_cparams("arbitrary"),
        name="dattn_sample",
    )(page_table.reshape(-1), *_pattn_operands(qbd, k_new, v_new, cache_k, cache_v, page_table, lam, subln_g, tn))


def _pattn_operands(qbd, k_new, v_new, cache_k, cache_v, page_table, lam, subln_g, tn):
    page0, per_page, newb = _pattn_consts(tn, page_table.shape[1] * PAGE_SIZE)
    return (lam.reshape(1), qbd, k_new, v_new, subln_g.reshape(1, A_DV), page0, per_page, newb, cache_k, cache_v)


ROUTER_ROWS = 8 + N_EXPERTS


def _merge_stages(x_ref, ret_ref, dif_ref, ga0_ref, ga1_ref, gb0_ref, gb1_ref, wro_ref, wdo_ref, wo_ref, gf_ref,
                  wr_ref, br_ref, h_ref, xn_ref, re_ref, rw_ref):
    live = {}

    def retention_branch():
        ga = jnp.concatenate([ga0_ref[...], ga1_ref[...]], axis=1).astype(F32)
        live["a"] = jax.nn.sigmoid(ga) * jnp.dot(ret_ref[...], wro_ref[...], preferred_element_type=F32)

    def diff_branch():
        gb = jnp.concatenate([gb0_ref[...], gb1_ref[...]], axis=1).astype(F32)
        d = jnp.dot(dif_ref[...], wdo_ref[...], preferred_element_type=F32)
        live["m"] = (live.pop("a") + jax.nn.sigmoid(gb) * d).astype(BF16)

    def residual():
        h_ref[...] = x_ref[...] + jnp.dot(live.pop("m"), wo_ref[...], preferred_element_type=F32)

    def route():
        _route(h_ref[...], gf_ref, wr_ref, br_ref, xn_ref, re_ref, rw_ref)

    return [retention_branch, diff_branch, residual, route]


def _merge_kernel(*refs):
    for stage in _merge_stages(*refs):
        stage()


def _route(h1, gf_ref, wr_ref, br_ref, xn_ref, re_ref, rw_ref):
    ms = jnp.mean(h1 * h1, axis=-1, keepdims=True)
    xn = h1 * lax.rsqrt(ms + EPS) * gf_ref[...]
    xn_ref[...] = xn

    lt = lax.dot_general(wr_ref[...], xn.astype(BF16), _NT, preferred_element_type=F32) + br_ref[:, 0:1]
    bm = lt.shape[1]
    row8 = lax.broadcasted_iota(jnp.int32, (8, bm), 0)

    def first_argmax(v):
        top = jnp.max(v, axis=0, keepdims=True)
        return top, jnp.min(jnp.where(v == top, row8, 8), axis=0, keepdims=True)

    lg = jnp.where(row8 < N_GROUPS, lt[0:8], NEG_BIG)
    gmax, g_sel = first_argmax(lg)
    g_w = 1.0 / jnp.sum(jnp.exp(lg - gmax), axis=0, keepdims=True)
    el = jnp.zeros((8, bm), F32)
    for g in range(N_GROUPS):
        el = jnp.where(g_sel == g, lt[8 + 8 * g:16 + 8 * g], el)
    ee = jnp.exp(el - jnp.max(el, axis=0, keepdims=True))
    ep = ee / jnp.sum(ee, axis=0, keepdims=True)
    p1, i1 = first_argmax(ep)
    p2, i2 = first_argmax(jnp.where(row8 == i1, -1.0, ep))
    den = p1 + p2
    re_ref[0:1, :] = g_sel * EXP_PER_GROUP + i1
    re_ref[1:2, :] = g_sel * EXP_PER_GROUP + i2
    rw_ref[0:1, :] = g_w * p1 / den
    rw_ref[1:2, :] = g_w * p2 / den


MERGE_BM = 512
N_MERGE_IN, N_MERGE_OUT = 13, 4


def _merge_specs(m, wrap):
    bm = min(m, MERGE_BM)
    const = lambda shape: pl.BlockSpec(shape, wrap(lambda i: (0,) * len(shape)))
    rows = lambda width, col=0: pl.BlockSpec((bm, width), wrap(lambda i: (i, col)))
    half = D_MODEL // 2
    in_specs = [rows(D_MODEL), rows(R_V), rows(A_V),
                rows(half, OFF_GA // half), rows(half, OFF_GA // half + 1),
                rows(half, OFF_GB // half), rows(half, OFF_GB // half + 1),
                const((R_V, D_MODEL)), const((A_V, D_MODEL)), const((D_MODEL, D_MODEL)),
                const((1, D_MODEL)), const((ROUTER_ROWS, D_MODEL)), const((ROUTER_ROWS, 128))]
    cols = lambda: pl.BlockSpec((2, bm), wrap(lambda i: (0, i)))
    out_specs = [rows(D_MODEL), rows(D_MODEL), cols(), cols()]
    out_shape = [jax.ShapeDtypeStruct((m, D_MODEL), F32), jax.ShapeDtypeStruct((m, D_MODEL), F32),
                 jax.ShapeDtypeStruct((2, m), jnp.int32), jax.ShapeDtypeStruct((2, m), F32)]
    return m // bm, in_specs, out_specs, out_shape


def _merge_operands(x, ret, dif, proj, w_ret_out, w_diff_out, w_out, norm_ffn, w_router, b_router):
    return (x, ret, dif, proj, proj, proj, proj, w_ret_out, w_diff_out, w_out, norm_ffn.reshape(1, D_MODEL),
            w_router, b_router)


def _merge(*args):
    steps, in_specs, out_specs, out_shape = _merge_specs(args[0].shape[0], lambda f: f)
    return pl.pallas_call(
        _merge_kernel, grid=(steps,), in_specs=in_specs, out_specs=out_specs, out_shape=out_shape,
        compiler_params=_cparams("parallel"), name="merge",
    )(*_merge_operands(*args))


def _merge_with_dattn_sample(merge_args, pattn_args, tn, lam_init):
    page_table = pattn_args[5]
    db, n_pages = page_table.shape
    wrap = lambda f: lambda i, pt: f(i)
    steps, m_in, m_out, m_shape = _merge_specs(merge_args[0].shape[0], wrap)
    assert steps == db
    p_in, p_out, p_shape, scratch, static = _pattn_specs(db, n_pages, tn, wrap)
    grid_spec = pltpu.PrefetchScalarGridSpec(num_scalar_prefetch=1, grid=(db,), in_specs=m_in + p_in,
                                             out_specs=m_out + [p_out], scratch_shapes=scratch)
    outs = pl.pallas_call(
        functools.partial(_merge_pattn_kernel, lam_init=lam_init, **static),
        grid_spec=grid_spec,
        out_shape=m_shape + [p_shape],
        compiler_params=_cparams("arbitrary"),
        name="merge_dattn_sample",
    )(page_table.reshape(-1), *_merge_operands(*merge_args), *_pattn_operands(*pattn_args, tn))
    return outs[:N_MERGE_OUT], outs[N_MERGE_OUT]


def _plan_kernel(re_ref, tri_ref, rank_ref, cnt_ref, carry_scr):
    @pl.when(pl.program_id(0) == 0)
    def _():
        carry_scr[...] = jnp.zeros_like(carry_scr)

    tb = re_ref.shape[1]
    erow = lax.broadcasted_iota(jnp.int32, (N_EXPERTS, tb), 0)
    oh0 = jnp.where(erow == re_ref[0:1, :], 1.0, 0.0)
    oh1 = jnp.where(erow == re_ref[1:2, :], 1.0, 0.0)
    both = oh0 + oh1
    before = jnp.dot(both.astype(BF16), tri_ref[...], preferred_element_type=F32) + carry_scr[:, 0:1]
    rank_ref[0:1, :] = jnp.sum(oh0 * before, axis=0, keepdims=True).astype(jnp.int32)
    rank_ref[1:2, :] = jnp.sum(oh1 * before, axis=0, keepdims=True).astype(jnp.int32)
    carry_scr[...] = carry_scr[...] + jnp.sum(both, axis=1, keepdims=True)
    cnt_ref[...] = carry_scr[...]


def _plan(expert_id):
    m = expert_id.shape[1]
    tb = min(m, 512)
    tri = jnp.asarray(np.triu(np.ones((tb, tb), np.float32), 1), BF16)
    return pl.pallas_call(
        _plan_kernel,
        grid=(m // tb,),
        in_specs=[pl.BlockSpec((2, tb), lambda i: (0, i)),
                  pl.BlockSpec((tb, tb), lambda i: (0, 0))],
        out_specs=[pl.BlockSpec((2, tb), lambda i: (0, i)),
                   pl.BlockSpec((N_EXPERTS, 128), lambda i: (0, 0))],
        out_shape=[jax.ShapeDtypeStruct((2, m), jnp.int32),
                   jax.ShapeDtypeStruct((N_EXPERTS, 128), F32)],
        scratch_shapes=[pltpu.VMEM((N_EXPERTS, 128), F32)],
        compiler_params=_cparams("arbitrary"),
        name="moe_plan",
    )(expert_id, tri)


def _dispatch_kernel(dest_ref, pad_ref, nu_ref, x_ref, xs_ref, zero_scr, sem, zsem):
    tb = x_ref.shape[0]

    def row_copy(t, s):
        return pltpu.make_async_copy(x_ref.at[pl.ds(t, 1)], xs_ref.at[pl.ds(dest_ref[s, t], 1)], sem)

    def issue(t, carry):
        row_copy(t, 0).start()
        row_copy(t, 1).start()
        return carry

    @pl.when(pl.program_id(0) == 0)
    def _():
        zero_scr[...] = jnp.zeros_like(zero_scr)

        def zero_chunk(e, r):
            off = pl.multiple_of(pad_ref[0, e] + r * 8, 8)
            return pltpu.make_async_copy(zero_scr.at[pl.ds(0, 8)], xs_ref.at[pl.ds(off, 8)], zsem)

        def pad_expert(e, carry):
            n = pad_ref[1, e]
            lax.fori_loop(0, n, lambda r, c: (zero_chunk(e, r).start(), c)[1], 0)
            lax.fori_loop(0, n, lambda r, c: (zero_chunk(e, r).wait(), c)[1], 0)
            return carry

        lax.fori_loop(0, N_EXPERTS, pad_expert, 0)
        nblk = xs_ref.shape[0] // MOE_TM
        zero_blk = lambda i: pltpu.make_async_copy(
            zero_scr, xs_ref.at[pl.ds(pl.multiple_of(i * MOE_TM, MOE_TM), MOE_TM)], zsem)
        lax.fori_loop(nu_ref[0], nblk, lambda i, c: (zero_blk(i).start(), c)[1], 0)
        lax.fori_loop(nu_ref[0], nblk, lambda i, c: (zero_blk(i).wait(), c)[1], 0)

    lax.fori_loop(0, tb, issue, 0, unroll=8)

    def drain(t, carry):
        row_copy(t, 0).wait()
        row_copy(t, 1).wait()
        return carry

    lax.fori_loop(0, tb, drain, 0, unroll=8)


def _dispatch(xn, dest, pad_info, n_used, rows):
    m = xn.shape[0]
    tb = min(m, 512)
    return pl.pallas_call(
        _dispatch_kernel,
        grid=(m // tb,),
        in_specs=[pl.BlockSpec((2, tb), lambda i: (0, i), memory_space=pltpu.SMEM),
                  pl.BlockSpec(memory_space=pltpu.SMEM),
                  pl.BlockSpec(memory_space=pltpu.SMEM),
                  pl.BlockSpec((tb, D_MODEL), lambda i: (i, 0))],
        out_specs=pl.BlockSpec(memory_space=pl.ANY),
        out_shape=jax.ShapeDtypeStruct((rows, D_MODEL), F32),
        scratch_shapes=[pltpu.VMEM((MOE_TM, D_MODEL), F32),
                        pltpu.SemaphoreType.DMA(()), pltpu.SemaphoreType.DMA(())],
        compiler_params=_cparams("arbitrary"),
        name="moe_dispatch",
    )(dest, pad_info, n_used, xn)


def _expert_kernel(be_ref, nu_ref, x_ref, wg_ref, wu_ref, wd_ref, y_ref):
    live = pl.program_id(0) < nu_ref[0]

    @pl.when(jnp.logical_not(live))
    def _():
        y_ref[...] = jnp.zeros_like(y_ref)

    @pl.when(live)
    def _():
        x = x_ref[...].astype(BF16)
        gate = jnp.dot(x, wg_ref[...], preferred_element_type=F32)
        up = jnp.dot(x, wu_ref[...], preferred_element_type=F32)
        hid = (gate * jax.nn.sigmoid(gate)) * up
        y_ref[...] = jnp.dot(hid.astype(BF16), wd_ref[...], preferred_element_type=F32)


def _experts(xs, block_e, n_used, w_gate, w_up, w_down):
    rows = xs.shape[0]
    nblk = rows // MOE_TM
    live = lambda i, be, nu: jnp.minimum(i, nu[0] - 1)
    grid_spec = pltpu.PrefetchScalarGridSpec(
        num_scalar_prefetch=2,
        grid=(nblk,),
        in_specs=[pl.BlockSpec((MOE_TM, D_MODEL), lambda i, be, nu: (live(i, be, nu), 0)),
                  pl.BlockSpec((None, D_MODEL, D_EXPERT), lambda i, be, nu: (be[live(i, be, nu)], 0, 0)),
                  pl.BlockSpec((None, D_MODEL, D_EXPERT), lambda i, be, nu: (be[live(i, be, nu)], 0, 0)),
                  pl.BlockSpec((None, D_EXPERT, D_MODEL), lambda i, be, nu: (be[live(i, be, nu)], 0, 0))],
        out_specs=pl.BlockSpec((MOE_TM, D_MODEL), lambda i, be, nu: (i, 0)),
    )
    return pl.pallas_call(
        _expert_kernel,
        grid_spec=grid_spec,
        out_shape=jax.ShapeDtypeStruct((rows, D_MODEL), F32),
        compiler_params=_cparams("arbitrary"),
        name="moe_experts",
    )(block_e, n_used, xs, w_gate, w_up, w_down)


def _combine_kernel(dest_ref, dnext_ref, h_ref, w_ref, gf_ref, yb_ref, y_ref, g_scr, sem):
    tb = h_ref.shape[0]
    i = pl.program_id(0)
    slot = i % 2

    def row_copy(d_ref, sl, t, s):
        return pltpu.make_async_copy(yb_ref.at[pl.ds(d_ref[s, t], 1)], g_scr.at[sl, s, pl.ds(t, 1)], sem.at[sl])

    def gather(d_ref, sl, wait):
        def body(t, carry):
            for s in range(TOP_K):
                cp = row_copy(d_ref, sl, t, s)
                cp.wait() if wait else cp.start()
            return carry

        lax.fori_loop(0, tb, body, 0, unroll=8)

    @pl.when(i == 0)
    def _():
        gather(dest_ref, 0, False)

    @pl.when(i + 1 < pl.num_programs(0))
    def _():
        gather(dnext_ref, 1 - slot, False)

    gather(dest_ref, slot, True)
    y = h_ref[...] + (w_ref[:, 0:1] * g_scr[slot, 0] + w_ref[:, 1:2] * g_scr[slot, 1])
    ms = jnp.mean(y * y, axis=-1, keepdims=True)
    y_ref[...] = y * lax.rsqrt(ms + EPS) * gf_ref[...]


def _combine(h1, yb, dest, weights_t, norm_final):
    m = h1.shape[0]
    tb = min(m, 256)
    last = m // tb - 1
    return pl.pallas_call(
        _combine_kernel,
        grid=(m // tb,),
        in_specs=[pl.BlockSpec((2, tb), lambda i: (0, i), memory_space=pltpu.SMEM),
                  pl.BlockSpec((2, tb), lambda i: (0, jnp.minimum(i + 1, last)), memory_space=pltpu.SMEM),
                  pl.BlockSpec((tb, D_MODEL), lambda i: (i, 0)),
                  pl.BlockSpec((tb, 2), lambda i: (i, 0)),
                  pl.BlockSpec((1, D_MODEL), lambda i: (0, 0)),
                  pl.BlockSpec(memory_space=pl.ANY)],
        out_specs=pl.BlockSpec((tb, D_MODEL), lambda i: (i, 0)),
        out_shape=jax.ShapeDtypeStruct((m, D_MODEL), F32),
        scratch_shapes=[pltpu.VMEM((2, TOP_K, tb, D_MODEL), F32), pltpu.SemaphoreType.DMA((2,))],
        compiler_params=_cparams("arbitrary"),
        name="moe_combine",
    )(dest, dest, h1, weights_t, norm_final.reshape(1, D_MODEL), yb)


def _moe_and_final(h1, xn, expert_id, weights, w_gate, w_up, w_down, norm_final):
    m = h1.shape[0]
    rank, counts = _plan(expert_id)
    counts = counts[:, 0].astype(jnp.int32)
    padded = (counts + MOE_TM - 1) // MOE_TM * MOE_TM
    pad_end = jnp.cumsum(padded)
    pad_start = pad_end - padded
    is_e = expert_id[:, :, None] == jnp.arange(N_EXPERTS, dtype=jnp.int32)
    dest = jnp.sum(jnp.where(is_e, pad_start, 0), axis=-1) + rank
    nblk = (m * TOP_K) // MOE_TM + N_EXPERTS
    block_start = jnp.arange(nblk, dtype=jnp.int32) * MOE_TM
    block_e = jnp.minimum(jnp.sum((block_start[:, None] >= pad_end[None, :]).astype(jnp.int32), axis=1),
                          N_EXPERTS - 1).astype(jnp.int32)
    n_used = (pad_end[-1:] // MOE_TM).astype(jnp.int32)
    zero_from = (pad_start + counts) // 8 * 8
    pad_info = jnp.stack([zero_from, (pad_end - zero_from) // 8]).astype(jnp.int32)
    xs = _dispatch(xn, dest, pad_info, n_used, nblk * MOE_TM)
    yb = _experts(xs, block_e, n_used, w_gate, w_up, w_down)
    return _combine(h1, yb, dest, weights.T, norm_final)


def kernel(x_prompt, x_sample, state_ret, cache_k, cache_v, page_table, norm_mix, w_in, lambda_q1, lambda_k1,
           lambda_q2, lambda_k2, subln_g, w_ret_out, w_diff_out, w_out, norm_ffn, w_group, b_group, w_expert,
           b_expert, w_up, w_gate, w_down, norm_final):
    b, t_seq, _ = x_prompt.shape
    db, tn, _ = x_sample.shape
    depth = w_in.shape[0]
    assert depth == 1 and tn <= 8
    n_phys = cache_k.shape[1]
    l = 0
    lam_init = 0.8 - 0.6 * float(np.exp(-0.3 * l))
    lam = (jnp.exp(jnp.sum(lambda_q1[l].astype(F32) * lambda_k1[l].astype(F32)))
           - jnp.exp(jnp.sum(lambda_q2[l].astype(F32) * lambda_k2[l].astype(F32))) + lam_init)

    w_in_b = w_in[l].astype(BF16)
    wkt_b = w_in[l][:, OFF_AK:OFF_AK + A_QK].T.astype(BF16)
    wro, wdo, wo = w_ret_out[l].astype(BF16), w_diff_out[l].astype(BF16), w_out[l].astype(BF16)
    wg_b, wu_b, wd_b = w_gate[l].astype(BF16), w_up[l].astype(BF16), w_down[l].astype(BF16)
    w_router = jnp.zeros((ROUTER_ROWS, D_MODEL), F32)
    w_router = w_router.at[0:N_GROUPS].set(w_group[l].T).at[8:].set(w_expert[l].T).astype(BF16)
    b_router = jnp.zeros((ROUTER_ROWS,), F32).at[0:N_GROUPS].set(b_group[l]).at[8:].set(b_expert[l])
    b_router = jnp.broadcast_to(b_router[:, None], (ROUTER_ROWS, 128))

    merge_args = lambda x, ret, dif, proj: (x, ret, dif, proj, wro, wdo, wo, norm_ffn[l], w_router, b_router)
    moe = lambda h1, xn, eid, wts: _moe_and_final(h1, xn, eid, wts, wg_b, wu_b, wd_b, norm_final)

    xp = x_prompt.reshape(b * t_seq, D_MODEL)
    proj_p, kt_p, vr_p = _proj(xp, norm_mix[l], w_in_b, wkt_b, seq_len=t_seq)
    ret_p, s_p = _ret_prompt(proj_p, b, t_seq)
    dif_p = _dattn_prompt(proj_p, lam, subln_g[l], b, t_seq, lam_init)

    xs = x_sample.reshape(db * tn, D_MODEL)
    proj_s, k_s, v_s = _proj(xs, norm_mix[l], w_in_b, wkt_b)
    ret_s, s_s = _ret_sample(proj_s, state_ret[l], tn)
    aq = proj_s[:, OFF_AQ:OFF_AQ + A_QK].astype(F32).reshape(db, tn, A_HEADS, 2, A_DH) * (A_DH ** -0.5)
    eye_h = jnp.eye(A_HEADS, dtype=F32)
    eye_c = jnp.eye(2, dtype=F32)
    qbd = jnp.einsum('bqhcd,hH,cC->bCHqhcd', aq, eye_h, eye_c).reshape(db, 2 * A_HEADS * tn, A_QK).astype(BF16)
    pad_new = lambda a: jnp.pad(a.reshape(db, tn, -1), ((0, 0), (0, 8 - tn), (0, 0)))
    ck_pages = jnp.transpose(cache_k[l], (0, 2, 3, 4, 1)).reshape(n_phys, A_QK, PAGE_SIZE)
    cv_pages = cache_v[l].reshape(n_phys, PAGE_SIZE * A_HEADS, A_DV)
    pattn_args = (qbd, pad_new(k_s), pad_new(v_s), ck_pages, cv_pages, page_table.astype(jnp.int32), lam, subln_g[l])

    if (b * t_seq) // min(b * t_seq, MERGE_BM) == db:
        merged_p, dif_s = _merge_with_dattn_sample(merge_args(xp, ret_p, dif_p, proj_p), pattn_args, tn, lam_init)
    else:
        merged_p = _merge(*merge_args(xp, ret_p, dif_p, proj_p))
        dif_s = _dattn_sample(*pattn_args, tn, lam_init)
    y_p = moe(*merged_p)
    y_s = moe(*_merge(*merge_args(xs, ret_s, dif_s.reshape(db * tn, A_V).astype(BF16), proj_s)))

    return (y_p.reshape(b, t_seq, D_MODEL),
            y_s.reshape(db, tn, D_MODEL),
            s_p[None],
            s_s[None],
            jnp.transpose(kt_p.reshape(1, b, A_HEADS, 2, A_DH, t_seq), (0, 1, 5, 2, 3, 4)),
            vr_p.reshape(1, b, t_seq, A_HEADS, A_DV),
            k_s.reshape(1, db, tn, A_HEADS, 2, A_DH),
            v_s.reshape(1, db, tn, A_HEADS, A_DV))
```

```python
import functools

import numpy as np
import jax
import jax.numpy as jnp
from jax import lax
from jax.experimental import pallas as pl
from jax.experimental.pallas import tpu as pltpu

D_MODEL = 1024
R_HEADS, R_DK, R_DV = 4, 128, 256
R_CHUNK = 256
A_HEADS, A_DH, A_DV = 4, 64, 128
N_GROUPS, EXP_PER_GROUP, TOP_K, D_EXPERT = 4, 8, 2, 512
N_EXPERTS = N_GROUPS * EXP_PER_GROUP
PAGE_SIZE = 128
EPS, SUBLN_EPS, NEG_BIG = 1e-6, 1e-5, -1e30

R_QK, R_V = R_HEADS * R_DK, R_HEADS * R_DV
A_QK, A_V = A_HEADS * 2 * A_DH, A_HEADS * A_DV
OFF_RQ, OFF_RK, OFF_RV, OFF_RG = 0, R_QK, 2 * R_QK, 2 * R_QK + R_V
OFF_AQ = OFF_RG + R_V
OFF_AK, OFF_AV = OFF_AQ + A_QK, OFF_AQ + 2 * A_QK
OFF_GA = OFF_AV + A_V
OFF_GB = OFF_GA + D_MODEL
N_IN = OFF_GB + D_MODEL

PROJ_BM = 512
PROJ_BN = 512
ATTN_T = 256
PAGES_PER_STEP = 16
MOE_TM = 512
COMBINE_TB = 512
VMEM_LIMIT = 56 * 1024 * 1024

F32, BF16 = jnp.float32, jnp.bfloat16
_NT = (((1,), (1,)), ((), ()))
_TN = (((0,), (0,)), ((), ()))


def _cparams(*sem):
    return pltpu.CompilerParams(dimension_semantics=sem, vmem_limit_bytes=VMEM_LIMIT)


def _log_decay():
    return np.log1p(-np.exp2(-5.0 - np.arange(R_HEADS, dtype=np.float64)))


def _alibi_slopes():
    return np.exp2(-8.0 * np.arange(1, A_HEADS + 1, dtype=np.float64) / A_HEADS)


def _proj_kernel(x_ref, g_ref, w_ref, wkt_ref, p_ref, k_ref, v_ref, xn_ref, *, seq_major):
    x = x_ref[...]
    ms = jnp.mean(x * x, axis=-1, keepdims=True)
    xn_ref[...] = (x * lax.rsqrt(ms + EPS) * g_ref[...]).astype(BF16)
    for j in range(N_IN // PROJ_BN):
        cols = slice(j * PROJ_BN, (j + 1) * PROJ_BN)
        acc = jnp.dot(xn_ref[...], w_ref[:, cols], preferred_element_type=F32)
        p_ref[:, cols] = acc.astype(BF16)
        if j == OFF_AK // PROJ_BN:
            if seq_major:
                k_ref[...] = lax.dot_general(wkt_ref[...], xn_ref[...], _NT, preferred_element_type=F32)
            else:
                k_ref[...] = acc
        if j == OFF_AV // PROJ_BN:
            if seq_major:
                for h in range(A_HEADS):
                    v_ref[pl.ds(h, acc.shape[0], stride=A_HEADS), :] = acc[:, h * A_DV:(h + 1) * A_DV]
            else:
                v_ref[...] = acc


def _proj(x, gain, w_bf16, wkt_bf16, seq_len=None):
    m = x.shape[0]
    bm = min(m, PROJ_BM) if seq_len is None else min(seq_len, PROJ_BM)
    if seq_len is None:
        kv_specs = [pl.BlockSpec((bm, A_QK), lambda i: (i, 0)),
                    pl.BlockSpec((bm, A_V), lambda i: (i, 0))]
        kv_shapes = [jax.ShapeDtypeStruct((m, A_QK), F32), jax.ShapeDtypeStruct((m, A_V), F32)]
    else:
        per_seq = seq_len // bm
        kv_specs = [pl.BlockSpec((None, A_QK, bm), lambda i: (i // per_seq, 0, i % per_seq)),
                    pl.BlockSpec((bm * A_HEADS, A_DV), lambda i: (i, 0))]
        kv_shapes = [jax.ShapeDtypeStruct((m // seq_len, A_QK, seq_len), F32),
                     jax.ShapeDtypeStruct((m * A_HEADS, A_DV), F32)]
    resident = lambda shape: pl.BlockSpec(shape, lambda i: (0, 0), pipeline_mode=pl.Buffered(1))
    return pl.pallas_call(
        functools.partial(_proj_kernel, seq_major=seq_len is not None),
        grid=(m // bm,),
        in_specs=[pl.BlockSpec((bm, D_MODEL), lambda i: (i, 0)),
                  resident((1, D_MODEL)), resident((D_MODEL, N_IN)), resident((A_QK, D_MODEL))],
        out_specs=[pl.BlockSpec((bm, N_IN), lambda i: (i, 0))] + kv_specs,
        out_shape=[jax.ShapeDtypeStruct((m, N_IN), BF16)] + kv_shapes,
        scratch_shapes=[pltpu.VMEM((bm, D_MODEL), BF16)],
        compiler_params=_cparams("parallel"),
        name="proj",
    )(x, gain.reshape(1, D_MODEL), w_bf16, wkt_bf16)


def _ret_consts(c, rows_per_seq):
    lg = _log_decay()
    pos = np.arange(c, dtype=np.float64)
    dist = pos[:, None] - pos[None, :]
    n = c * rows_per_seq
    intra = np.zeros((R_HEADS, n, n), np.float32)
    qdec = np.zeros((n, R_QK), np.float32)
    kdec = np.zeros((n, R_QK), np.float32)
    for h in range(R_HEADS):
        blk = np.where(dist >= 0, np.exp(np.maximum(dist, 0.0) * lg[h]), 0.0)
        for s in range(rows_per_seq):
            intra[h, s * c:(s + 1) * c, s * c:(s + 1) * c] = blk
            qdec[s * c:(s + 1) * c, h * R_DK:(h + 1) * R_DK] = np.exp((pos + 1.0) * lg[h])[:, None]
            kdec[s * c:(s + 1) * c, h * R_DK:(h + 1) * R_DK] = (
                np.exp((c - 1.0 - pos) * lg[h])[:, None] * R_DK ** -0.5)
    cdec = [float(np.float32(np.exp(c * lg[h]))) for h in range(R_HEADS)]
    return jnp.asarray(intra), jnp.asarray(qdec), jnp.asarray(kdec), cdec


def _gated_norm(o, g):
    ms = jnp.mean(o * o, axis=-1, keepdims=True)
    return (o * lax.rsqrt(ms + EPS)) * (g * jax.nn.sigmoid(g))


def _ret_prompt_kernel(q_ref, k_ref, v_ref, rg_ref, intra_ref, qdec_ref, kdec_ref,
                       ret_ref, sout_ref, s_scr, *, cdec):
    c = pl.program_id(1)

    @pl.when(c == 0)
    def _():
        s_scr[...] = jnp.zeros_like(s_scr)

    q = q_ref[...]
    kf = k_ref[...].astype(F32)
    qd = (q.astype(F32) * qdec_ref[...]).astype(BF16)
    kd = (kf * kdec_ref[...]).astype(BF16)
    ks = (kf * (R_DK ** -0.5)).astype(BF16)
    for h in range(R_HEADS):
        hk = slice(h * R_DK, (h + 1) * R_DK)
        hv = slice(h * R_DV, (h + 1) * R_DV)
        vh = v_ref[:, hv]
        s_old = s_scr[h]
        sc = lax.dot_general(q[:, hk], ks[:, hk], _NT, preferred_element_type=F32) * intra_ref[h]
        o = (jnp.dot(sc.astype(BF16), vh, preferred_element_type=F32)
             + jnp.dot(qd[:, hk], s_old.astype(BF16), preferred_element_type=F32))
        s_scr[h] = cdec[h] * s_old + lax.dot_general(kd[:, hk], vh, _TN, preferred_element_type=F32)
        ret_ref[:, hv] = _gated_norm(o, rg_ref[:, hv].astype(F32)).astype(BF16)

    @pl.when(c == pl.num_programs(1) - 1)
    def _():
        sout_ref[0] = s_scr[...]


def _ret_prompt(proj, b, t):
    c = min(R_CHUNK, t)
    nc = t // c
    intra, qdec, kdec, cdec = _ret_consts(c, 1)
    row = lambda bi, ci: bi * nc + ci
    return pl.pallas_call(
        functools.partial(_ret_prompt_kernel, cdec=cdec),
        grid=(b, nc),
        in_specs=[pl.BlockSpec((c, R_QK), lambda bi, ci: (row(bi, ci), OFF_RQ // R_QK)),
                  pl.BlockSpec((c, R_QK), lambda bi, ci: (row(bi, ci), OFF_RK // R_QK)),
                  pl.BlockSpec((c, R_V), lambda bi, ci: (row(bi, ci), OFF_RV // R_V)),
                  pl.BlockSpec((c, R_V), lambda bi, ci: (row(bi, ci), OFF_RG // R_V)),
                  pl.BlockSpec((R_HEADS, c, c), lambda bi, ci: (0, 0, 0)),
                  pl.BlockSpec((c, R_QK), lambda bi, ci: (0, 0)),
                  pl.BlockSpec((c, R_QK), lambda bi, ci: (0, 0))],
        out_specs=[pl.BlockSpec((c, R_V), lambda bi, ci: (row(bi, ci), 0)),
                   pl.BlockSpec((1, R_HEADS, R_DK, R_DV), lambda bi, ci: (bi, 0, 0, 0))],
        out_shape=[jax.ShapeDtypeStruct((b * t, R_V), BF16),
                   jax.ShapeDtypeStruct((b, R_HEADS, R_DK, R_DV), F32)],
        scratch_shapes=[pltpu.VMEM((R_HEADS, R_DK, R_DV), F32)],
        compiler_params=_cparams("parallel", "arbitrary"),
        name="ret_prompt",
    )(proj, proj, proj, proj, intra, qdec, kdec)


def _ret_sample_kernel(q_ref, k_ref, v_ref, rg_ref, s_ref, intra_ref, qdec_ref, kdec_ref,
                       ret_ref, sout_ref, *, cdec, tn, nb):
    q = q_ref[...]
    kf = k_ref[...].astype(F32)
    qd = (q.astype(F32) * qdec_ref[...]).astype(BF16)
    kdf = kf * kdec_ref[...]
    ks = (kf * (R_DK ** -0.5)).astype(BF16)
    rows = lax.broadcasted_iota(jnp.int32, (nb * tn, 1), 0)
    for h in range(R_HEADS):
        hk = slice(h * R_DK, (h + 1) * R_DK)
        hv = slice(h * R_DV, (h + 1) * R_DV)
        vh = v_ref[:, hv]
        sc = lax.dot_general(q[:, hk], ks[:, hk], _NT, preferred_element_type=F32) * intra_ref[h]
        o = jnp.dot(sc.astype(BF16), vh, preferred_element_type=F32)
        for s in range(nb):
            mine = (rows >= s * tn) & (rows < (s + 1) * tn)
            s_old = s_ref[s, h]
            o = o + jnp.where(mine, jnp.dot(qd[:, hk], s_old.astype(BF16), preferred_element_type=F32), 0.0)
            kd_s = jnp.where(mine, kdf[:, hk], 0.0).astype(BF16)
            sout_ref[s, h] = cdec[h] * s_old + lax.dot_general(kd_s, vh, _TN, preferred_element_type=F32)
        ret_ref[:, hv] = _gated_norm(o, rg_ref[:, hv].astype(F32)).astype(BF16)


def _ret_sample(proj, state, tn):
    db = state.shape[0]
    nb = min(db, 8)
    n = nb * tn
    intra, qdec, kdec, cdec = _ret_consts(tn, nb)
    return pl.pallas_call(
        functools.partial(_ret_sample_kernel, cdec=cdec, tn=tn, nb=nb),
        grid=(db // nb,),
        in_specs=[pl.BlockSpec((n, R_QK), lambda i: (i, OFF_RQ // R_QK)),
                  pl.BlockSpec((n, R_QK), lambda i: (i, OFF_RK // R_QK)),
                  pl.BlockSpec((n, R_V), lambda i: (i, OFF_RV // R_V)),
                  pl.BlockSpec((n, R_V), lambda i: (i, OFF_RG // R_V)),
                  pl.BlockSpec((nb, R_HEADS, R_DK, R_DV), lambda i: (i, 0, 0, 0)),
                  pl.BlockSpec((R_HEADS, n, n), lambda i: (0, 0, 0)),
                  pl.BlockSpec((n, R_QK), lambda i: (0, 0)),
                  pl.BlockSpec((n, R_QK), lambda i: (0, 0))],
        out_specs=[pl.BlockSpec((n, R_V), lambda i: (i, 0)),
                   pl.BlockSpec((nb, R_HEADS, R_DK, R_DV), lambda i: (i, 0, 0, 0))],
        out_shape=[jax.ShapeDtypeStruct((db * tn, R_V), BF16),
                   jax.ShapeDtypeStruct(state.shape, F32)],
        compiler_params=_cparams("parallel"),
        name="ret_sample",
    )(proj, proj, proj, proj, state, intra, qdec, kdec)


def _subln(o, g, lam_init):
    ms = jnp.mean(o * o, axis=-1, keepdims=True)
    return (o * lax.rsqrt(ms + SUBLN_EPS)) * g * (1.0 - lam_init)


def _dattn_prompt_kernel(lam_ref, q_ref, k_ref, v_ref, gt_ref, rc_ref, posk_ref, posq_ref, o_ref,
                         qq_scr, vt_scr, s_scr, m_scr, l_scr, acc_scr, *, t, lam_init, slopes):
    qi = pl.program_id(1)
    lam = lam_ref[0]
    heads = [slice(h * A_DV, (h + 1) * A_DV) for h in range(A_HEADS)]

    @pl.when(qi == 0)
    def _():
        for h in range(A_HEADS):
            for kj in range(v_ref.shape[0] // t):
                vt_scr[h, kj] = jnp.transpose(v_ref[kj * t:(kj + 1) * t, heads[h]].astype(F32)).astype(BF16)

    rc = rc_ref[...]
    for h in range(A_HEADS):
        qt = jnp.transpose(q_ref[:, heads[h]].astype(F32) * (A_DH ** -0.5))
        feat = lax.broadcasted_iota(jnp.int32, qt.shape, 0)
        qq_scr[h, :, 0:t] = jnp.where(feat < A_DH, qt, 0.0).astype(BF16)
        qq_scr[h, :, t:2 * t] = jnp.where(feat >= A_DH, qt, 0.0).astype(BF16)
    m_scr[...] = jnp.full_like(m_scr, NEG_BIG)
    l_scr[...] = jnp.zeros_like(l_scr)
    acc_scr[...] = jnp.zeros_like(acc_scr)

    def tile(kj, dist, masked):
        rows = pl.ds(pl.multiple_of(kj * t, t), t)
        posk = posk_ref[...]
        for h in range(A_HEADS):
            s_scr[h] = jnp.dot(jnp.concatenate([k_ref[rows, heads[h]], posk], axis=1),
                               jnp.concatenate([qq_scr[h], posq_ref[h]], axis=0), preferred_element_type=F32)
        p_all, alpha_all = [], []
        causal = jnp.concatenate([rc, rc], axis=1) >= 0.0 if masked else None
        for h in range(A_HEADS):
            s = s_scr[h]
            if masked:
                s = jnp.where(causal, s, NEG_BIG)
            off = slopes[h] * dist
            m_prev = m_scr[h]
            m_new = jnp.maximum(m_prev, jnp.max(s, axis=0, keepdims=True) - off)
            alpha = jnp.exp(m_prev - m_new)
            p = jnp.exp(s - (m_new + off))
            l_scr[h] = alpha * l_scr[h] + jnp.sum(p, axis=0, keepdims=True)
            m_scr[h] = m_new
            p_all.append(p.astype(BF16))
            alpha_all.append(alpha)
        for h in range(A_HEADS):
            acc_scr[h] = alpha_all[h] * acc_scr[h] + jnp.dot(vt_scr[h, kj], p_all[h], preferred_element_type=F32)

    def body(kj, carry):
        tile(kj, ((qi - kj) * t).astype(F32), False)
        return carry

    lax.fori_loop(0, qi, body, 0)
    tile(qi, 0.0, True)

    for h in range(A_HEADS):
        o = acc_scr[h] / l_scr[h]
        out = o[:, 0:t] - lam * o[:, t:2 * t]
        ms = jnp.mean(out * out, axis=0, keepdims=True)
        out = (out * lax.rsqrt(ms + SUBLN_EPS)) * gt_ref[...] * (1.0 - lam_init)
        o_ref[:, heads[h]] = jnp.transpose(out).astype(BF16)


def _dattn_prompt(proj, lam, subln_g, b, t_seq, lam_init):
    t = min(ATTN_T, t_seq)
    nq = t_seq // t
    r = np.arange(t, dtype=np.float32)
    rc = jnp.asarray(r[None, :] - r[:, None])
    gt = jnp.broadcast_to(subln_g.astype(F32)[:, None], (A_DV, t))
    slopes = [float(s) for s in _alibi_slopes()]
    assert t <= 256
    posk = np.zeros((t, 128), np.float32)
    posk[:, 0], posk[:, 1] = 1.0, r
    posq = np.zeros((A_HEADS, 128, 2 * t), np.float32)
    for h in range(A_HEADS):
        posq[h, 0], posq[h, 1] = -slopes[h] * np.concatenate([r, r]), slopes[h]
    return pl.pallas_call(
        functools.partial(_dattn_prompt_kernel, t=t, lam_init=lam_init, slopes=slopes),
        grid=(b, nq),
        in_specs=[pl.BlockSpec(memory_space=pltpu.SMEM),
                  pl.BlockSpec((t, A_QK), lambda bi, qi: (bi * nq + qi, OFF_AQ // A_QK)),
                  pl.BlockSpec((t_seq, A_QK), lambda bi, qi: (bi, OFF_AK // A_QK)),
                  pl.BlockSpec((t_seq, A_V), lambda bi, qi: (bi, OFF_AV // A_V)),
                  pl.BlockSpec((A_DV, t), lambda bi, qi: (0, 0)),
                  pl.BlockSpec((t, t), lambda bi, qi: (0, 0)),
                  pl.BlockSpec((t, 128), lambda bi, qi: (0, 0)),
                  pl.BlockSpec((A_HEADS, 128, 2 * t), lambda bi, qi: (0, 0, 0))],
        out_specs=pl.BlockSpec((t, A_V), lambda bi, qi: (bi * nq + qi, 0)),
        out_shape=jax.ShapeDtypeStruct((b * t_seq, A_V), BF16),
        scratch_shapes=[pltpu.VMEM((A_HEADS, A_DV, 2 * t), BF16),
                        pltpu.VMEM((A_HEADS, nq, A_DV, t), BF16),
                        pltpu.VMEM((A_HEADS, t, 2 * t), F32),
                        pltpu.VMEM((A_HEADS, 1, 2 * t), F32),
                        pltpu.VMEM((A_HEADS, 1, 2 * t), F32),
                        pltpu.VMEM((A_HEADS, A_DV, 2 * t), F32)],
        compiler_params=_cparams("parallel", "arbitrary"),
        name="dattn_prompt",
    )(lam.reshape(1), proj, proj, proj, gt, rc, jnp.asarray(posk, BF16), jnp.asarray(posq, BF16))


def _pattn_consts(tn, past_len):
    slopes = _alibi_slopes()
    nrow = 2 * A_HEADS * tn
    slope_row = np.array([slopes[(r // tn) % A_HEADS] for r in range(nrow)])[:, None]
    q_row = np.array([r % tn for r in range(nrow)], np.float64)[:, None]
    key = np.arange(PAGE_SIZE, dtype=np.float64)[None, :]
    page0 = slope_row * key - slope_row * (past_len + q_row)
    per_page = np.broadcast_to(slope_row * PAGE_SIZE, (nrow, PAGE_SIZE))
    new = np.where((key < tn) & (key <= q_row), -slope_row * (q_row - key), NEG_BIG)
    return jnp.asarray(page0, F32), jnp.asarray(per_page, F32), jnp.asarray(new, F32)


def _pattn_kernel(pt_ref, lam_ref, qbd_ref, kn_ref, vn_ref, g_ref, page0_ref, perpage_ref, newb_ref, ck_ref, cv_ref,
                  o_ref, kbuf, vbuf, sem, kn_scr, vn_scr, m_scr, l_scr, acc_scr, *, npg, tn, lam_init):
    j = pl.program_id(1)
    nj = pl.num_programs(1)
    step = pl.program_id(0) * nj + j
    slot = step % 2

    def page_copies(st, sl):
        cps = []
        for p_i in range(npg):
            pg = pt_ref[st * npg + p_i]
            cps.append(pltpu.make_async_copy(ck_ref.at[pg], kbuf.at[sl, p_i], sem.at[sl]))
            cps.append(pltpu.make_async_copy(cv_ref.at[pg], vbuf.at[sl, p_i], sem.at[sl]))
        return cps

    @pl.when(step == 0)
    def _():
        for cp in page_copies(0, 0):
            cp.start()

    @pl.when(step + 1 < pl.num_programs(0) * nj)
    def _():
        for cp in page_copies(step + 1, 1 - slot):
            cp.start()

    qbd = qbd_ref[...]
    nrow = qbd.shape[0]
    wide = lambda x: jnp.concatenate([x] * A_HEADS, axis=1)

    @pl.when(step == 0)
    def _():
        kn_scr[...] = jnp.zeros_like(kn_scr)
        vn_scr[...] = jnp.zeros_like(vn_scr)

    @pl.when(j == 0)
    def _():
        kn_scr[0:8, :] = kn_ref[...]
        vn_scr[0:8, :] = vn_ref[...]
        newb = newb_ref[...]
        s = lax.dot_general(qbd, kn_scr[...].astype(BF16), _NT, preferred_element_type=F32)
        s = jnp.where(newb > 0.5 * NEG_BIG, s + newb, NEG_BIG)
        m0 = jnp.broadcast_to(jnp.max(s, axis=1, keepdims=True), s.shape)
        p = jnp.exp(s - m0)
        m_scr[...] = m0
        l_scr[...] = p
        acc_scr[...] = jnp.dot(p.astype(BF16), vn_scr[...].astype(BF16), preferred_element_type=F32)

    for cp in page_copies(step, slot):
        cp.wait()

    page0 = page0_ref[...]
    per_page = perpage_ref[...]
    s_all = []
    for p_i in range(npg):
        page = (j * npg + p_i).astype(F32)
        s = jnp.dot(qbd, kbuf[slot, p_i].astype(BF16), preferred_element_type=F32)
        s_all.append(s + (page0 + per_page * page))
    mx = s_all[0]
    for s in s_all[1:]:
        mx = jnp.maximum(mx, s)
    m_prev = m_scr[...]
    m_new = jnp.maximum(m_prev, jnp.broadcast_to(jnp.max(mx, axis=1, keepdims=True), mx.shape))
    alpha = jnp.exp(m_prev - m_new)
    lsum = jnp.zeros((nrow, PAGE_SIZE), F32)
    pv = [jnp.zeros((nrow, A_DV), F32) for _ in range(A_HEADS)]
    for p_i in range(npg):
        p = jnp.exp(s_all[p_i] - m_new)
        lsum = lsum + p
        pb = p.astype(BF16)
        for h in range(A_HEADS):
            vh = vbuf[slot, p_i, pl.ds(h, PAGE_SIZE, stride=A_HEADS), :].astype(BF16)
            pv[h] = pv[h] + jnp.dot(pb, vh, preferred_element_type=F32)
    m_scr[...] = m_new
    l_scr[...] = alpha * l_scr[...] + lsum
    acc_scr[...] = wide(alpha) * acc_scr[...] + jnp.concatenate(pv, axis=1)

    @pl.when(j == nj - 1)
    def _():
        l_tot = jnp.sum(l_scr[...], axis=1, keepdims=True)
        lam = lam_ref[0]
        g = g_ref[...]
        half = A_HEADS * tn
        for h in range(A_HEADS):
            cols = slice(h * A_DV, (h + 1) * A_DV)
            blk = acc_scr[:, cols] / l_tot
            out = blk[h * tn:(h + 1) * tn] - lam * blk[half + h * tn:half + (h + 1) * tn]
            o_ref[:, cols] = _subln(out, g, lam_init)


def _dattn_sample(qbd, k_new, v_new, cache_k, cache_v, page_table, lam, subln_g, tn, lam_init):
    db, n_pages = page_table.shape
    npg = min(PAGES_PER_STEP, n_pages)
    nrow = 2 * A_HEADS * tn
    assert nrow % 16 == 0 and tn <= 8
    page0, per_page, newb = _pattn_consts(tn, n_pages * PAGE_SIZE)
    const = lambda shape: pl.BlockSpec(shape, lambda bi, j, pt: (0,) * len(shape))
    grid_spec = pltpu.PrefetchScalarGridSpec(
        num_scalar_prefetch=1,
        grid=(db, n_pages // npg),
        in_specs=[pl.BlockSpec(memory_space=pltpu.SMEM),
                  pl.BlockSpec((None, nrow, A_QK), lambda bi, j, pt: (bi, 0, 0)),
                  pl.BlockSpec((None, 8, A_QK), lambda bi, j, pt: (bi, 0, 0)),
                  pl.BlockSpec((None, 8, A_V), lambda bi, j, pt: (bi, 0, 0)),
                  const((1, A_DV)), const((nrow, PAGE_SIZE)), const((nrow, PAGE_SIZE)), const((nrow, PAGE_SIZE)),
                  pl.BlockSpec(memory_space=pl.ANY), pl.BlockSpec(memory_space=pl.ANY)],
        out_specs=pl.BlockSpec((None, tn, A_V), lambda bi, j, pt: (bi, 0, 0)),
        scratch_shapes=[pltpu.VMEM((2, npg, A_QK, PAGE_SIZE), F32),
                        pltpu.VMEM((2, npg, PAGE_SIZE * A_HEADS, A_DV), F32),
                        pltpu.SemaphoreType.DMA((2,)),
                        pltpu.VMEM((PAGE_SIZE, A_QK), F32), pltpu.VMEM((PAGE_SIZE, A_V), F32),
                        pltpu.VMEM((nrow, PAGE_SIZE), F32), pltpu.VMEM((nrow, PAGE_SIZE), F32),
                        pltpu.VMEM((nrow, A_V), F32)],
    )
    return pl.pallas_call(
        functools.partial(_pattn_kernel, npg=npg, tn=tn, lam_init=lam_init),
        grid_spec=grid_spec,
        out_shape=jax.ShapeDtypeStruct((db, tn, A_V), F32),
        compiler_params=_cparams("arbitrary", "arbitrary"),
        name="dattn_sample",
    )(page_table.reshape(-1), lam.reshape(1), qbd, k_new, v_new, subln_g.reshape(1, A_DV), page0, per_page, newb,
      cache_k, cache_v)


ROUTER_ROWS = 8 + N_EXPERTS


def _merge_kernel(x_ref, ret_ref, dif_ref, ga0_ref, ga1_ref, gb0_ref, gb1_ref, wro_ref, wdo_ref, wo_ref, gf_ref,
                  wr_ref, br_ref, h_ref, xn_ref, re_ref, rw_ref):
    a = jnp.dot(ret_ref[...], wro_ref[...], preferred_element_type=F32)
    d = jnp.dot(dif_ref[...], wdo_ref[...], preferred_element_type=F32)
    ga = jnp.concatenate([ga0_ref[...], ga1_ref[...]], axis=1).astype(F32)
    gb = jnp.concatenate([gb0_ref[...], gb1_ref[...]], axis=1).astype(F32)
    m = jax.nn.sigmoid(ga) * a + jax.nn.sigmoid(gb) * d
    h1 = x_ref[...] + jnp.dot(m.astype(BF16), wo_ref[...], preferred_element_type=F32)
    h_ref[...] = h1
    ms = jnp.mean(h1 * h1, axis=-1, keepdims=True)
    xn = h1 * lax.rsqrt(ms + EPS) * gf_ref[...]
    xn_ref[...] = xn

    lt = lax.dot_general(wr_ref[...], xn.astype(BF16), _NT, preferred_element_type=F32) + br_ref[:, 0:1]
    bm = lt.shape[1]
    row8 = lax.broadcasted_iota(jnp.int32, (8, bm), 0)

    def first_argmax(v):
        top = jnp.max(v, axis=0, keepdims=True)
        return top, jnp.min(jnp.where(v == top, row8, 8), axis=0, keepdims=True)

    lg = jnp.where(row8 < N_GROUPS, lt[0:8], NEG_BIG)
    gmax, g_sel = first_argmax(lg)
    g_w = 1.0 / jnp.sum(jnp.exp(lg - gmax), axis=0, keepdims=True)
    el = jnp.zeros((8, bm), F32)
    for g in range(N_GROUPS):
        el = jnp.where(g_sel == g, lt[8 + 8 * g:16 + 8 * g], el)
    ee = jnp.exp(el - jnp.max(el, axis=0, keepdims=True))
    ep = ee / jnp.sum(ee, axis=0, keepdims=True)
    p1, i1 = first_argmax(ep)
    p2, i2 = first_argmax(jnp.where(row8 == i1, -1.0, ep))
    den = p1 + p2
    re_ref[0:1, :] = g_sel * EXP_PER_GROUP + i1
    re_ref[1:2, :] = g_sel * EXP_PER_GROUP + i2
    rw_ref[0:1, :] = g_w * p1 / den
    rw_ref[1:2, :] = g_w * p2 / den


def _merge(x, ret, dif, proj, w_ret_out, w_diff_out, w_out, norm_ffn, w_router, b_router):
    m = x.shape[0]
    bm = min(m, 512)
    const = lambda shape: pl.BlockSpec(shape, lambda i: (0,) * len(shape))
    half = D_MODEL // 2
    gate = lambda off: pl.BlockSpec((bm, half), lambda i: (i, off // half))
    return pl.pallas_call(
        _merge_kernel,
        grid=(m // bm,),
        in_specs=[pl.BlockSpec((bm, D_MODEL), lambda i: (i, 0)),
                  pl.BlockSpec((bm, R_V), lambda i: (i, 0)),
                  pl.BlockSpec((bm, A_V), lambda i: (i, 0)),
                  gate(OFF_GA), gate(OFF_GA + half), gate(OFF_GB), gate(OFF_GB + half),
                  const((R_V, D_MODEL)), const((A_V, D_MODEL)), const((D_MODEL, D_MODEL)),
                  const((1, D_MODEL)), const((ROUTER_ROWS, D_MODEL)), const((ROUTER_ROWS, 128))],
        out_specs=[pl.BlockSpec((bm, D_MODEL), lambda i: (i, 0)),
                   pl.BlockSpec((bm, D_MODEL), lambda i: (i, 0)),
                   pl.BlockSpec((2, bm), lambda i: (0, i)),
                   pl.BlockSpec((2, bm), lambda i: (0, i))],
        out_shape=[jax.ShapeDtypeStruct((m, D_MODEL), F32),
                   jax.ShapeDtypeStruct((m, D_MODEL), F32),
                   jax.ShapeDtypeStruct((2, m), jnp.int32),
                   jax.ShapeDtypeStruct((2, m), F32)],
        compiler_params=_cparams("parallel"),
        name="merge",
    )(x, ret, dif, proj, proj, proj, proj, w_ret_out, w_diff_out, w_out, norm_ffn.reshape(1, D_MODEL),
      w_router, b_router)


def _plan_kernel(re_ref, tri_ref, rank_ref, cnt_ref, carry_scr):
    @pl.when(pl.program_id(0) == 0)
    def _():
        carry_scr[...] = jnp.zeros_like(carry_scr)

    tb = re_ref.shape[1]
    erow = lax.broadcasted_iota(jnp.int32, (N_EXPERTS, tb), 0)
    oh0 = jnp.where(erow == re_ref[0:1, :], 1.0, 0.0)
    oh1 = jnp.where(erow == re_ref[1:2, :], 1.0, 0.0)
    both = oh0 + oh1
    before = jnp.dot(both.astype(BF16), tri_ref[...], preferred_element_type=F32) + carry_scr[:, 0:1]
    rank_ref[0:1, :] = jnp.sum(oh0 * before, axis=0, keepdims=True).astype(jnp.int32)
    rank_ref[1:2, :] = jnp.sum(oh1 * before, axis=0, keepdims=True).astype(jnp.int32)
    carry_scr[...] = carry_scr[...] + jnp.sum(both, axis=1, keepdims=True)
    cnt_ref[...] = carry_scr[...]


def _plan(expert_id):
    m = expert_id.shape[1]
    tb = min(m, 512)
    tri = jnp.asarray(np.triu(np.ones((tb, tb), np.float32), 1), BF16)
    return pl.pallas_call(
        _plan_kernel,
        grid=(m // tb,),
        in_specs=[pl.BlockSpec((2, tb), lambda i: (0, i)),
                  pl.BlockSpec((tb, tb), lambda i: (0, 0))],
        out_specs=[pl.BlockSpec((2, tb), lambda i: (0, i)),
                   pl.BlockSpec((N_EXPERTS, 128), lambda i: (0, 0))],
        out_shape=[jax.ShapeDtypeStruct((2, m), jnp.int32),
                   jax.ShapeDtypeStruct((N_EXPERTS, 128), F32)],
        scratch_shapes=[pltpu.VMEM((N_EXPERTS, 128), F32)],
        compiler_params=_cparams("arbitrary"),
        name="moe_plan",
    )(expert_id, tri)


def _dispatch_kernel(dest_ref, pad_ref, nu_ref, x_ref, xs_ref, zero_scr, sem, zsem):
    tb = x_ref.shape[0]

    def row_copy(t, s):
        return pltpu.make_async_copy(x_ref.at[pl.ds(t, 1)], xs_ref.at[pl.ds(dest_ref[s, t], 1)], sem)

    def issue(t, carry):
        row_copy(t, 0).start()
        row_copy(t, 1).start()
        return carry

    @pl.when(pl.program_id(0) == 0)
    def _():
        zero_scr[...] = jnp.zeros_like(zero_scr)

        def zero_chunk(e, r):
            off = pl.multiple_of(pad_ref[0, e] + r * 8, 8)
            return pltpu.make_async_copy(zero_scr.at[pl.ds(0, 8)], xs_ref.at[pl.ds(off, 8)], zsem)

        def pad_expert(e, carry):
            n = pad_ref[1, e]
            lax.fori_loop(0, n, lambda r, c: (zero_chunk(e, r).start(), c)[1], 0)
            lax.fori_loop(0, n, lambda r, c: (zero_chunk(e, r).wait(), c)[1], 0)
            return carry

        lax.fori_loop(0, N_EXPERTS, pad_expert, 0)
        nblk = xs_ref.shape[0] // MOE_TM
        zero_blk = lambda i: pltpu.make_async_copy(
            zero_scr, xs_ref.at[pl.ds(pl.multiple_of(i * MOE_TM, MOE_TM), MOE_TM)], zsem)
        lax.fori_loop(nu_ref[0], nblk, lambda i, c: (zero_blk(i).start(), c)[1], 0)
        lax.fori_loop(nu_ref[0], nblk, lambda i, c: (zero_blk(i).wait(), c)[1], 0)

    lax.fori_loop(0, tb, issue, 0, unroll=8)

    def drain(t, carry):
        row_copy(t, 0).wait()
        row_copy(t, 1).wait()
        return carry

    lax.fori_loop(0, tb, drain, 0, unroll=8)


def _dispatch(xn, dest, pad_info, n_used, rows):
    m = xn.shape[0]
    tb = min(m, 512)
    return pl.pallas_call(
        _dispatch_kernel,
        grid=(m // tb,),
        in_specs=[pl.BlockSpec((2, tb), lambda i: (0, i), memory_space=pltpu.SMEM),
                  pl.BlockSpec(memory_space=pltpu.SMEM),
                  pl.BlockSpec(memory_space=pltpu.SMEM),
                  pl.BlockSpec((tb, D_MODEL), lambda i: (i, 0))],
        out_specs=pl.BlockSpec(memory_space=pl.ANY),
        out_shape=jax.ShapeDtypeStruct((rows, D_MODEL), F32),
        scratch_shapes=[pltpu.VMEM((MOE_TM, D_MODEL), F32),
                        pltpu.SemaphoreType.DMA(()), pltpu.SemaphoreType.DMA(())],
        compiler_params=_cparams("arbitrary"),
        name="moe_dispatch",
    )(dest, pad_info, n_used, xn)


def _expert_kernel(be_ref, nu_ref, x_ref, wg_ref, wu_ref, wd_ref, y_ref):
    live = pl.program_id(0) < nu_ref[0]

    @pl.when(jnp.logical_not(live))
    def _():
        y_ref[...] = jnp.zeros_like(y_ref)

    @pl.when(live)
    def _():
        x = x_ref[...].astype(BF16)
        gate = jnp.dot(x, wg_ref[...], preferred_element_type=F32)
        up = jnp.dot(x, wu_ref[...], preferred_element_type=F32)
        hid = (gate * jax.nn.sigmoid(gate)) * up
        y_ref[...] = jnp.dot(hid.astype(BF16), wd_ref[...], preferred_element_type=F32)


def _experts(xs, block_e, n_used, w_gate, w_up, w_down):
    rows = xs.shape[0]
    nblk = rows // MOE_TM
    live = lambda i, be, nu: jnp.minimum(i, nu[0] - 1)
    grid_spec = pltpu.PrefetchScalarGridSpec(
        num_scalar_prefetch=2,
        grid=(nblk,),
        in_specs=[pl.BlockSpec((MOE_TM, D_MODEL), lambda i, be, nu: (live(i, be, nu), 0)),
                  pl.BlockSpec((None, D_MODEL, D_EXPERT), lambda i, be, nu: (be[live(i, be, nu)], 0, 0)),
                  pl.BlockSpec((None, D_MODEL, D_EXPERT), lambda i, be, nu: (be[live(i, be, nu)], 0, 0)),
                  pl.BlockSpec((None, D_EXPERT, D_MODEL), lambda i, be, nu: (be[live(i, be, nu)], 0, 0))],
        out_specs=pl.BlockSpec((MOE_TM, D_MODEL), lambda i, be, nu: (i, 0)),
    )
    return pl.pallas_call(
        _expert_kernel,
        grid_spec=grid_spec,
        out_shape=jax.ShapeDtypeStruct((rows, D_MODEL), F32),
        compiler_params=_cparams("arbitrary"),
        name="moe_experts",
    )(block_e, n_used, xs, w_gate, w_up, w_down)


def _combine_kernel(dest_ref, dnext_ref, h_ref, w_ref, gf_ref, yb_ref, y_ref, g_scr, sem):
    tb = h_ref.shape[0]
    i = pl.program_id(0)
    slot = i % 2

    def row_copy(d_ref, sl, t, s):
        return pltpu.make_async_copy(yb_ref.at[pl.ds(d_ref[s, t], 1)], g_scr.at[sl, s, pl.ds(t, 1)], sem.at[sl])

    def gather(d_ref, sl, wait):
        def body(t, carry):
            for s in range(TOP_K):
                cp = row_copy(d_ref, sl, t, s)
                cp.wait() if wait else cp.start()
            return carry

        lax.fori_loop(0, tb, body, 0, unroll=8)

    @pl.when(i == 0)
    def _():
        gather(dest_ref, 0, False)

    @pl.when(i + 1 < pl.num_programs(0))
    def _():
        gather(dnext_ref, 1 - slot, False)

    gather(dest_ref, slot, True)
    y = h_ref[...] + (w_ref[:, 0:1] * g_scr[slot, 0] + w_ref[:, 1:2] * g_scr[slot, 1])
    ms = jnp.mean(y * y, axis=-1, keepdims=True)
    y_ref[...] = y * lax.rsqrt(ms + EPS) * gf_ref[...]


def _combine(h1, yb, dest, weights_t, norm_final):
    m = h1.shape[0]
    tb = min(m, COMBINE_TB)
    last = m // tb - 1
    return pl.pallas_call(
        _combine_kernel,
        grid=(m // tb,),
        in_specs=[pl.BlockSpec((2, tb), lambda i: (0, i), memory_space=pltpu.SMEM),
                  pl.BlockSpec((2, tb), lambda i: (0, jnp.minimum(i + 1, last)), memory_space=pltpu.SMEM),
                  pl.BlockSpec((tb, D_MODEL), lambda i: (i, 0)),
                  pl.BlockSpec((tb, 2), lambda i: (i, 0)),
                  pl.BlockSpec((1, D_MODEL), lambda i: (0, 0)),
                  pl.BlockSpec(memory_space=pl.ANY)],
        out_specs=pl.BlockSpec((tb, D_MODEL), lambda i: (i, 0)),
        out_shape=jax.ShapeDtypeStruct((m, D_MODEL), F32),
        scratch_shapes=[pltpu.VMEM((2, TOP_K, tb, D_MODEL), F32), pltpu.SemaphoreType.DMA((2,))],
        compiler_params=_cparams("arbitrary"),
        name="moe_combine",
    )(dest, dest, h1, weights_t, norm_final.reshape(1, D_MODEL), yb)


def _moe_and_final(h1, xn, expert_id, weights, w_gate, w_up, w_down, norm_final):
    m = h1.shape[0]
    rank, counts = _plan(expert_id)
    counts = counts[:, 0].astype(jnp.int32)
    padded = (counts + MOE_TM - 1) // MOE_TM * MOE_TM
    pad_end = jnp.cumsum(padded)
    pad_start = pad_end - padded
    is_e = expert_id[:, :, None] == jnp.arange(N_EXPERTS, dtype=jnp.int32)
    dest = jnp.sum(jnp.where(is_e, pad_start, 0), axis=-1) + rank
    nblk = (m * TOP_K) // MOE_TM + N_EXPERTS
    block_start = jnp.arange(nblk, dtype=jnp.int32) * MOE_TM
    block_e = jnp.minimum(jnp.sum((block_start[:, None] >= pad_end[None, :]).astype(jnp.int32), axis=1),
                          N_EXPERTS - 1).astype(jnp.int32)
    n_used = (pad_end[-1:] // MOE_TM).astype(jnp.int32)
    zero_from = (pad_start + counts) // 8 * 8
    pad_info = jnp.stack([zero_from, (pad_end - zero_from) // 8]).astype(jnp.int32)
    xs = _dispatch(xn, dest, pad_info, n_used, nblk * MOE_TM)
    yb = _experts(xs, block_e, n_used, w_gate, w_up, w_down)
    return _combine(h1, yb, dest, weights.T, norm_final)


def kernel(x_prompt, x_sample, state_ret, cache_k, cache_v, page_table, norm_mix, w_in, lambda_q1, lambda_k1,
           lambda_q2, lambda_k2, subln_g, w_ret_out, w_diff_out, w_out, norm_ffn, w_group, b_group, w_expert,
           b_expert, w_up, w_gate, w_down, norm_final):
    b, t_seq, _ = x_prompt.shape
    db, tn, _ = x_sample.shape
    depth = w_in.shape[0]
    assert depth == 1 and tn <= 8
    n_phys = cache_k.shape[1]
    l = 0
    lam_init = 0.8 - 0.6 * float(np.exp(-0.3 * l))
    lam = (jnp.exp(jnp.sum(lambda_q1[l].astype(F32) * lambda_k1[l].astype(F32)))
           - jnp.exp(jnp.sum(lambda_q2[l].astype(F32) * lambda_k2[l].astype(F32))) + lam_init)

    w_in_b = w_in[l].astype(BF16)
    wkt_b = w_in[l][:, OFF_AK:OFF_AK + A_QK].T.astype(BF16)
    wro, wdo, wo = w_ret_out[l].astype(BF16), w_diff_out[l].astype(BF16), w_out[l].astype(BF16)
    wg_b, wu_b, wd_b = w_gate[l].astype(BF16), w_up[l].astype(BF16), w_down[l].astype(BF16)
    w_router = jnp.zeros((ROUTER_ROWS, D_MODEL), F32)
    w_router = w_router.at[0:N_GROUPS].set(w_group[l].T).at[8:].set(w_expert[l].T).astype(BF16)
    b_router = jnp.zeros((ROUTER_ROWS,), F32).at[0:N_GROUPS].set(b_group[l]).at[8:].set(b_expert[l])
    b_router = jnp.broadcast_to(b_router[:, None], (ROUTER_ROWS, 128))

    def tail(x, ret, dif, proj):
        h1, xn, eid, wts = _merge(x, ret, dif, proj, wro, wdo, wo, norm_ffn[l], w_router, b_router)
        return _moe_and_final(h1, xn, eid, wts, wg_b, wu_b, wd_b, norm_final)

    xp = x_prompt.reshape(b * t_seq, D_MODEL)
    proj_p, kt_p, vr_p = _proj(xp, norm_mix[l], w_in_b, wkt_b, seq_len=t_seq)
    ret_p, s_p = _ret_prompt(proj_p, b, t_seq)
    dif_p = _dattn_prompt(proj_p, lam, subln_g[l], b, t_seq, lam_init)
    y_p = tail(xp, ret_p, dif_p, proj_p)

    xs = x_sample.reshape(db * tn, D_MODEL)
    proj_s, k_s, v_s = _proj(xs, norm_mix[l], w_in_b, wkt_b)
    ret_s, s_s = _ret_sample(proj_s, state_ret[l], tn)
    aq = proj_s[:, OFF_AQ:OFF_AQ + A_QK].astype(F32).reshape(db, tn, A_HEADS, 2, A_DH) * (A_DH ** -0.5)
    eye_h = jnp.eye(A_HEADS, dtype=F32)
    eye_c = jnp.eye(2, dtype=F32)
    qbd = jnp.einsum('bqhcd,hH,cC->bCHqhcd', aq, eye_h, eye_c).reshape(db, 2 * A_HEADS * tn, A_QK).astype(BF16)
    pad_new = lambda a: jnp.pad(a.reshape(db, tn, -1), ((0, 0), (0, 8 - tn), (0, 0)))
    ck_pages = jnp.transpose(cache_k[l], (0, 2, 3, 4, 1)).reshape(n_phys, A_QK, PAGE_SIZE)
    cv_pages = cache_v[l].reshape(n_phys, PAGE_SIZE * A_HEADS, A_DV)
    dif_s = _dattn_sample(qbd, pad_new(k_s), pad_new(v_s), ck_pages, cv_pages,
                          page_table.astype(jnp.int32), lam, subln_g[l], tn, lam_init)
    y_s = tail(xs, ret_s, dif_s.reshape(db * tn, A_V).astype(BF16), proj_s)

    return (y_p.reshape(b, t_seq, D_MODEL),
            y_s.reshape(db, tn, D_MODEL),
            s_p[None],
            s_s[None],
            jnp.transpose(kt_p.reshape(1, b, A_HEADS, 2, A_DH, t_seq), (0, 1, 5, 2, 3, 4)),
            vr_p.reshape(1, b, t_seq, A_HEADS, A_DV),
            k_s.reshape(1, db, tn, A_HEADS, 2, A_DH),
            v_s.reshape(1, db, tn, A_HEADS, A_DV))
```
